```python
import math
import jax, jax.numpy as jnp
from jax import lax
import numpy as np

D_MODEL = 2048
BATCH = 2
SEQ = 4096
DEPTH = 4
DEC_BATCH = 8
DEC_SEQ = 4
PAST_LEN = 16384
PAGE_SIZE = 128

N_META = 16
D_FF = 4096
GDN_HEADS = 8
GDN_DK = 128
GDN_DV = 128
GDN_CHUNK = 128
CONV_W = 4
DIFF_HEADS = 8
DIFF_DQK = 64
DIFF_DV = 2 * DIFF_DQK
ROPE_THETA = 500000.0
ROPE_DIM = DIFF_DQK // 4
Q_BLOCK = 128
EPS = 1e-6

GDN_QK_W = GDN_HEADS * GDN_DK
GDN_V_W = GDN_HEADS * GDN_DV
CONV_CH = 2 * GDN_QK_W + GDN_V_W
DIFF_Q_W = DIFF_HEADS * 2 * DIFF_DQK
DIFF_V_W = DIFF_HEADS * DIFF_DV
IN_SIZES = (CONV_CH, GDN_V_W, GDN_HEADS, GDN_HEADS, DIFF_Q_W, DIFF_Q_W, DIFF_V_W, D_MODEL, D_MODEL)
IN_COLS = sum(IN_SIZES)

kernel_name = "hybrid_gdn_diffattn_macaron_step"


def rmsnorm(x, g):
    xf = x.astype(jnp.float32)
    y = xf * lax.rsqrt(jnp.mean(xf * xf, axis=-1, keepdims=True) + EPS)
    return (y * g.astype(jnp.float32)).astype(x.dtype)


def l2norm(x):
    return x * lax.rsqrt(jnp.sum(x * x, axis=-1, keepdims=True) + EPS)


def half_ffn(x, norm, w_gate, w_up, w_down):
    h = rmsnorm(x, norm)
    return x + 0.5 * ((jax.nn.silu(h @ w_gate) * (h @ w_up)) @ w_down)


def split_in(p):
    offs = np.cumsum(IN_SIZES)[:-1].tolist()
    return jnp.split(p, offs, axis=-1)


def partial_rope(x, pos):
    half = ROPE_DIM // 2
    inv = ROPE_THETA ** (-jnp.arange(half, dtype=jnp.float32) / half)
    ang = pos.astype(jnp.float32)[:, None] * inv[None, :]
    bshape = (ang.shape[0],) + (1,) * (x.ndim - 3) + (half,)
    cos = jnp.cos(ang).reshape(bshape)
    sin = jnp.sin(ang).reshape(bshape)
    xr = x[..., :ROPE_DIM].astype(jnp.float32)
    x1, x2 = xr[..., :half], xr[..., half:]
    rot = jnp.concatenate([x1 * cos - x2 * sin, x2 * cos + x1 * sin], axis=-1).astype(x.dtype)
    return jnp.concatenate([rot, x[..., ROPE_DIM:]], axis=-1)


def causal_conv(buf, u, w):
    xp = jnp.concatenate([buf.astype(u.dtype), u], axis=1)
    L = u.shape[1]
    y = xp[:, 0:L] * w[0]
    for i in range(1, CONV_W):
        y = y + xp[:, i:i + L] * w[i]
    return jax.nn.silu(y), xp[:, -(CONV_W - 1):]


def gdn_prepare(qkv_c, b_raw, a_raw, a_log, dt_bias):
    B, L = qkv_c.shape[:2]
    q, k, v = jnp.split(qkv_c.astype(jnp.float32), [GDN_QK_W, 2 * GDN_QK_W], axis=-1)
    q = l2norm(q.reshape(B, L, GDN_HEADS, GDN_DK)) * GDN_DK ** -0.5
    k = l2norm(k.reshape(B, L, GDN_HEADS, GDN_DK))
    v = v.reshape(B, L, GDN_HEADS, GDN_DV)
    beta = jax.nn.sigmoid(b_raw.astype(jnp.float32))
    g = -jnp.exp(a_log.astype(jnp.float32)) * jax.nn.softplus(a_raw.astype(jnp.float32) + dt_bias.astype(jnp.float32))
    return q, k, v, g, beta


def gdn_chunk(S, q, k, v, g, beta):
    L = q.shape[1]
    G = jnp.cumsum(g, axis=1)
    Gh = jnp.moveaxis(G, 1, 2)
    idx = jnp.arange(L)
    incl = idx[:, None] >= idx[None, :]
    strict = idx[:, None] > idx[None, :]
    decay_incl = jnp.exp(jnp.where(incl, Gh[..., :, None] - Gh[..., None, :], -jnp.inf))
    decay_strict = jnp.where(strict, decay_incl, 0.0)
    kb = k * beta[..., None]
    A = jnp.einsum('bihd,bjhd->bhij', kb, k) * decay_strict
    gam = jnp.exp(G)[..., None]
    rhs = jnp.moveaxis(jnp.concatenate([kb * gam, v * beta[..., None]], axis=-1), 1, 2)
    sol = lax.linalg.triangular_solve(jnp.eye(L, dtype=jnp.float32) + A, rhs,
                                      left_side=True, lower=True, unit_diagonal=True)
    W, U0 = sol[..., :GDN_DK], sol[..., GDN_DK:]
    U = U0 - jnp.einsum('bhld,bhde->bhle', W, S)
    qk = jnp.einsum('bihd,bjhd->bhij', q, k) * decay_incl
    o = jnp.einsum('blhd,bhde->bhle', q * gam, S) + jnp.einsum('bhij,bhje->bhie', qk, U)
    S_new = (jnp.exp(G[:, -1])[..., None, None] * S
             + jnp.einsum('blhd,bhle->bhde', k * jnp.exp(G[:, -1:] - G)[..., None], U))
    return S_new, jnp.moveaxis(o, 1, 2)


def gdn_prompt(q, k, v, g, beta):
    B = q.shape[0]
    S = jnp.zeros((B, GDN_HEADS, GDN_DK, GDN_DV), jnp.float32)
    S, o_meta = gdn_chunk(S, q[:, :N_META], k[:, :N_META], v[:, :N_META], g[:, :N_META], beta[:, :N_META])

    def to_chunks(a):
        a = a[:, N_META:]
        return jnp.moveaxis(a.reshape((B, -1, GDN_CHUNK) + a.shape[2:]), 1, 0)

    S, o_rest = lax.scan(lambda s, c: gdn_chunk(s, *c), S,
                         (to_chunks(q), to_chunks(k), to_chunks(v), to_chunks(g), to_chunks(beta)))
    o_rest = jnp.moveaxis(o_rest, 0, 1).reshape(B, -1, GDN_HEADS, GDN_DV)
    return jnp.concatenate([o_meta, o_rest], axis=1), S


def diff_lambda(lam_vecs, lam_init):
    lv = lam_vecs.astype(jnp.float32)
    return jnp.exp(jnp.sum(lv[0] * lv[1])) - jnp.exp(jnp.sum(lv[2] * lv[3])) + lam_init


def diff_qkv(dq, dk, dv, pos):
    B, L = dq.shape[:2]
    q = partial_rope(dq.reshape(B, L, DIFF_HEADS, 2, DIFF_DQK), pos)
    k = partial_rope(dk.reshape(B, L, DIFF_HEADS, 2, DIFF_DQK), pos)
    v = dv.reshape(B, L, DIFF_HEADS, DIFF_DV)
    return q, k, v


def diff_core(q, k, v, mask, lam):
    s = jnp.einsum('bqhcd,bkhcd->bhcqk', q, k, preferred_element_type=jnp.float32) * DIFF_DQK ** -0.5
    p = jax.nn.softmax(jnp.where(mask, s, -jnp.inf), axis=-1)
    a = p[:, :, 0] - lam * p[:, :, 1]
    return jnp.einsum('bhqk,bkhd->bqhd', a, v.astype(jnp.float32))


def merge_branches(o_g, z, o_d, ga, gb, gdn_norm, subln, lam_init, w_bg, w_bd, w_o):
    B, L = z.shape[:2]
    dt = z.dtype
    y_g = rmsnorm(o_g.astype(dt), gdn_norm) * jax.nn.silu(z.reshape(B, L, GDN_HEADS, GDN_DV))
    y_d = rmsnorm(o_d.astype(dt), subln) * (1.0 - lam_init)
    m = (jax.nn.sigmoid(ga) * (y_g.reshape(B, L, GDN_V_W) @ w_bg)
         + jax.nn.sigmoid(gb) * (y_d.reshape(B, L, DIFF_V_W) @ w_bd))
    return m @ w_o


def mixer_prompt(h, pos, w_in, conv_w, a_log, dt_bias, gdn_norm, lam_vecs, subln, lam_init, w_bg, w_bd, w_o):
    B, T = h.shape[:2]
    qkv, z, b_raw, a_raw, dq, dk, dv, ga, gb = split_in(h @ w_in)
    qkv_c, conv_buf = causal_conv(jnp.zeros((B, CONV_W - 1, CONV_CH), h.dtype), qkv, conv_w)
    o_g, S = gdn_prompt(*gdn_prepare(qkv_c, b_raw, a_raw, a_log, dt_bias))
    q, k, v = diff_qkv(dq, dk, dv, pos)
    lam = diff_lambda(lam_vecs, lam_init)
    nb = -(-T // Q_BLOCK)
    Tp = nb * Q_BLOCK
    qp = jnp.pad(q, ((0, 0), (0, Tp - T), (0, 0), (0, 0), (0, 0)))
    kp = jnp.pad(k, ((0, 0), (0, Tp - T), (0, 0), (0, 0), (0, 0)))
    vp = jnp.pad(v, ((0, 0), (0, Tp - T), (0, 0), (0, 0)))
    kpos = jnp.arange(Tp)

    def block(i):
        qb = lax.dynamic_slice_in_dim(qp, i * Q_BLOCK, Q_BLOCK, axis=1)
        qi = i * Q_BLOCK + jnp.arange(Q_BLOCK)
        return diff_core(qb, kp, vp, qi[:, None] >= kpos[None, :], lam)

    o_d = lax.map(block, jnp.arange(nb))
    o_d = jnp.moveaxis(o_d, 0, 1).reshape(B, Tp, DIFF_HEADS, DIFF_DV)[:, :T]
    y = merge_branches(o_g, z, o_d, ga, gb, gdn_norm, subln, lam_init, w_bg, w_bd, w_o)
    return y, k, v, S, conv_buf


def mixer_sample(h, pos, k_cache, v_cache, page_table, S0, conv0, w_in, conv_w, a_log, dt_bias,
                 gdn_norm, lam_vecs, subln, lam_init, w_bg, w_bd, w_o):
    B, L = h.shape[:2]
    qkv, z, b_raw, a_raw, dq, dk, dv, ga, gb = split_in(h @ w_in)
    qkv_c, conv_buf = causal_conv(conv0, qkv, conv_w)
    S, o_g = gdn_chunk(S0.astype(jnp.float32), *gdn_prepare(qkv_c, b_raw, a_raw, a_log, dt_bias))
    q, k, v = diff_qkv(dq, dk, dv, pos)
    lam = diff_lambda(lam_vecs, lam_init)
    past = page_table.shape[1] * PAGE_SIZE
    k_past = k_cache[page_table].reshape(B, past, DIFF_HEADS, 2, DIFF_DQK)
    v_past = v_cache[page_table].reshape(B, past, DIFF_HEADS, DIFF_DV)
    k_all = jnp.concatenate([k_past.astype(k.dtype), k], axis=1)
    v_all = jnp.concatenate([v_past.astype(v.dtype), v], axis=1)
    mask = (past + jnp.arange(L))[:, None] >= jnp.arange(past + L)[None, :]
    o_d = diff_core(q, k_all, v_all, mask, lam)
    y = merge_branches(o_g, z, o_d, ga, gb, gdn_norm, subln, lam_init, w_bg, w_bd, w_o)
    return y, k, v, S, conv_buf


def setup_inputs(seed: int = 0) -> dict:
    key = jax.random.key(seed)
    ks = iter(jax.random.split(key, 48))

    def nrm(shape, scale):
        return jax.random.normal(next(ks), shape, jnp.float32) * scale

    def gain(shape):
        return 1.0 + nrm(shape, 0.02)

    n_pages = PAST_LEN // PAGE_SIZE
    n_used = DEC_BATCH * n_pages
    n_pool = n_used + (n_used + 3) // 4
    page_table = jax.random.permutation(next(ks), n_pool)[:n_used].reshape(DEC_BATCH, n_pages).astype(jnp.int32)
    dt = jnp.exp(jax.random.uniform(next(ks), (DEPTH, GDN_HEADS), jnp.float32, math.log(1e-3), math.log(1e-1)))
    return {
        "x_prompt": nrm((BATCH, SEQ, D_MODEL), 1.0),
        "x_sample": nrm((DEC_BATCH, DEC_SEQ, D_MODEL), 1.0),
        "cache_k": nrm((DEPTH, n_pool, PAGE_SIZE, DIFF_HEADS, 2, DIFF_DQK), 1.0),
        "cache_v": nrm((DEPTH, n_pool, PAGE_SIZE, DIFF_HEADS, DIFF_DV), 1.0),
        "state_delta": nrm((DEPTH, DEC_BATCH, GDN_HEADS, GDN_DK, GDN_DV), 0.1),
        "state_conv": nrm((DEPTH, DEC_BATCH, CONV_W - 1, CONV_CH), 1.0),
        "page_table": page_table,
        "meta_tokens": nrm((N_META, D_MODEL), 1.0),
        "ffn1_norm": gain((DEPTH, D_MODEL)),
        "ffn1_gate": nrm((DEPTH, D_MODEL, D_FF), D_MODEL ** -0.5),
        "ffn1_up": nrm((DEPTH, D_MODEL, D_FF), D_MODEL ** -0.5),
        "ffn1_down": nrm((DEPTH, D_FF, D_MODEL), D_FF ** -0.5),
        "mix_norm": gain((DEPTH, D_MODEL)),
        "w_in": nrm((DEPTH, D_MODEL, IN_COLS), D_MODEL ** -0.5),
        "conv_w": nrm((DEPTH, CONV_W, CONV_CH), CONV_W ** -0.5),
        "gdn_a_log": jnp.log(jax.random.uniform(next(ks), (DEPTH, GDN_HEADS), jnp.float32, 1.0, 16.0)),
        "gdn_dt_bias": dt + jnp.log(-jnp.expm1(-dt)),
        "gdn_out_norm": gain((DEPTH, GDN_DV)),
        "diff_lambda": nrm((DEPTH, 4, DIFF_DQK), 0.1),
        "diff_subln": gain((DEPTH, DIFF_DV)),
        "w_branch_gdn": nrm((DEPTH, GDN_V_W, D_MODEL), GDN_V_W ** -0.5),
        "w_branch_diff": nrm((DEPTH, DIFF_V_W, D_MODEL), DIFF_V_W ** -0.5),
        "w_out": nrm((DEPTH, D_MODEL, D_MODEL), D_MODEL ** -0.5),
        "ffn2_norm": gain((DEPTH, D_MODEL)),
        "ffn2_gate": nrm((DEPTH, D_MODEL, D_FF), D_MODEL ** -0.5),
        "ffn2_up": nrm((DEPTH, D_MODEL, D_FF), D_MODEL ** -0.5),
        "ffn2_down": nrm((DEPTH, D_FF, D_MODEL), D_FF ** -0.5),
        "final_norm": gain((D_MODEL,)),
    }


def reference(x_prompt, x_sample, cache_k, cache_v, state_delta, state_conv, page_table,
              meta_tokens, ffn1_norm, ffn1_gate, ffn1_up, ffn1_down, mix_norm, w_in, conv_w,
              gdn_a_log, gdn_dt_bias, gdn_out_norm, diff_lambda, diff_subln, w_branch_gdn,
              w_branch_diff, w_out, ffn2_norm, ffn2_gate, ffn2_up, ffn2_down, final_norm):
    B = x_prompt.shape[0]
    x = jnp.concatenate([jnp.broadcast_to(meta_tokens.astype(x_prompt.dtype)[None], (B, N_META, D_MODEL)), x_prompt], axis=1)
    pos_p = jnp.arange(x.shape[1])
    xs = x_sample
    past = page_table.shape[1] * PAGE_SIZE
    pos_s = past + jnp.arange(xs.shape[1])
    kp_l, vp_l, sp_l, cp_l, ks_l, vs_l, ss_l, cs_l = [], [], [], [], [], [], [], []
    for l in range(DEPTH):
        lam_init = 0.8 - 0.6 * math.exp(-0.3 * l)
        wts = (w_in[l], conv_w[l], gdn_a_log[l], gdn_dt_bias[l], gdn_out_norm[l], diff_lambda[l],
               diff_subln[l], lam_init, w_branch_gdn[l], w_branch_diff[l], w_out[l])
        x = half_ffn(x, ffn1_norm[l], ffn1_gate[l], ffn1_up[l], ffn1_down[l])
        y, k_new, v_new, S_new, c_new = mixer_prompt(rmsnorm(x, mix_norm[l]), pos_p, *wts)
        x = x + y
        x = half_ffn(x, ffn2_norm[l], ffn2_gate[l], ffn2_up[l], ffn2_down[l])
        kp_l.append(k_new); vp_l.append(v_new); sp_l.append(S_new); cp_l.append(c_new)
        xs = half_ffn(xs, ffn1_norm[l], ffn1_gate[l], ffn1_up[l], ffn1_down[l])
        y, k_new, v_new, S_new, c_new = mixer_sample(rmsnorm(xs, mix_norm[l]), pos_s, cache_k[l], cache_v[l],
                                                     page_table, state_delta[l], state_conv[l], *wts)
        xs = xs + y
        xs = half_ffn(xs, ffn2_norm[l], ffn2_gate[l], ffn2_up[l], ffn2_down[l])
        ks_l.append(k_new); vs_l.append(v_new); ss_l.append(S_new); cs_l.append(c_new)
    y_prompt = rmsnorm(x, final_norm)[:, N_META:]
    y_sample = rmsnorm(xs, final_norm)
    return (y_prompt, y_sample,
            jnp.stack(kp_l), jnp.stack(vp_l), jnp.stack(sp_l), jnp.stack(cp_l),
            jnp.stack(ks_l), jnp.stack(vs_l), jnp.stack(ss_l), jnp.stack(cs_l))
```

```python
import functools
import math

import jax
import jax.numpy as jnp
from jax import lax
from jax.experimental import pallas as pl
from jax.experimental.pallas import tpu as pltpu

N_META = 16
GDN_HEADS = 8
GDN_DK = 128
GDN_DV = 128
CONV_W = 4
DIFF_HEADS = 8
DIFF_DQK = 64
DIFF_DV = 128
ROPE_THETA = 500000.0
ROPE_DIM = DIFF_DQK // 4
PAGE_SIZE = 128
EPS = 1e-6

CHUNK = 128
LANES = 128
SUBLANES = 8
NEG = -1e30
VMEM_LIMIT = 56 * 1024 * 1024
PAGES_PER_STEP = 4

GDN_QK_W = GDN_HEADS * GDN_DK
GDN_V_W = GDN_HEADS * GDN_DV
CONV_CH = 2 * GDN_QK_W + GDN_V_W
DIFF_W = DIFF_HEADS * 2 * DIFF_DQK
COL_Z, COL_GA, COL_GB, COL_DQ, COL_DK, COL_DV = 3, 4, 6, 8, 9, 10

f32 = jnp.float32
bf16 = jnp.bfloat16


def _params(*sem):
    return pltpu.CompilerParams(dimension_semantics=sem, vmem_limit_bytes=VMEM_LIMIT)


def _pick_tile(n, target, mult=SUBLANES):
    best = None
    for t in range(mult, min(n, target) + 1, mult):
        if n % t == 0:
            best = t
    assert best is not None, (n, target, mult)
    return best


def _rms(x, w):
    return x * lax.rsqrt(jnp.mean(x * x, axis=-1, keepdims=True) + EPS) * w


def _dot(a, b):
    return jnp.dot(a.astype(bf16), b.astype(bf16), preferred_element_type=f32)


def _dot_nt(a, b):
    return lax.dot_general(a.astype(bf16), b.astype(bf16), (((1,), (1,)), ((), ())),
                           preferred_element_type=f32)


def _split2(x):
    hi = x.astype(bf16)
    lo = (x - hi.astype(f32)).astype(bf16)
    return hi, lo


def _dot3(a, b):
    ah, al = _split2(a)
    bh, bl = _split2(b)
    d = functools.partial(jnp.dot, preferred_element_type=f32)
    return d(ah, bh) + (d(al, bh) + d(ah, bl))


def _dot_exact_lhs(a_bf, x):
    x1 = x.astype(bf16)
    r1 = x - x1.astype(f32)
    x2 = r1.astype(bf16)
    x3 = (r1 - x2.astype(f32)).astype(bf16)
    d = functools.partial(jnp.dot, preferred_element_type=f32)
    return d(a_bf, x1) + (d(a_bf, x2) + d(a_bf, x3))


def _dot_exact_rhs(x, b_bf):
    x1 = x.astype(bf16)
    r1 = x - x1.astype(f32)
    x2 = r1.astype(bf16)
    x3 = (r1 - x2.astype(f32)).astype(bf16)
    d = functools.partial(jnp.dot, preferred_element_type=f32)
    return d(x1, b_bf) + (d(x2, b_bf) + d(x3, b_bf))


def _ffn_body(x_ref, nw_ref, wg_ref, wu_ref, wd_ref, o_ref, h_ref):
    j = pl.program_id(1)

    @pl.when(j == 0)
    def _():
        h_ref[...] = _rms(x_ref[...], nw_ref[...]).astype(bf16)
        o_ref[...] = jnp.zeros_like(o_ref)

    h = h_ref[...]
    g = jnp.dot(h, wg_ref[...], preferred_element_type=f32)
    u = jnp.dot(h, wu_ref[...], preferred_element_type=f32)
    a = (g * jax.nn.sigmoid(g) * u).astype(bf16)
    o_ref[...] += jnp.dot(a, wd_ref[...], preferred_element_type=f32)

    @pl.when(j == pl.num_programs(1) - 1)
    def _():
        o_ref[...] = x_ref[...] + 0.5 * o_ref[...]


def _ffn(x, nw, wg, wu, wd, layer):
    m, d = x.shape
    ff = wg.shape[-1]
    tm = _pick_tile(m, 768)
    tf = _pick_tile(ff, 512, LANES)
    return pl.pallas_call(
        _ffn_body,
        out_shape=jax.ShapeDtypeStruct((m, d), f32),
        grid=(m // tm, ff // tf),
        in_specs=[
            pl.BlockSpec((tm, d), lambda i, j: (i, 0)),
            pl.BlockSpec((None, 1, d), lambda i, j: (layer, 0, 0)),
            pl.BlockSpec((None, d, tf), lambda i, j: (layer, 0, j)),
            pl.BlockSpec((None, d, tf), lambda i, j: (layer, 0, j)),
            pl.BlockSpec((None, tf, d), lambda i, j: (layer, j, 0)),
        ],
        out_specs=pl.BlockSpec((tm, d), lambda i, j: (i, 0)),
        scratch_shapes=[pltpu.VMEM((tm, d), bf16)],
        compiler_params=_params("parallel", "arbitrary"),
        name="ffn",
    )(x, nw, wg, wu, wd)


def _inproj_body(x_ref, nw_ref, w_ref, wt_ref, o_ref, t_ref, h_ref):
    @pl.when(pl.program_id(1) == 0)
    def _():
        h = _rms(x_ref[...], nw_ref[...]).astype(bf16)
        h_ref[...] = h
        t_ref[...] = jnp.dot(h, wt_ref[...], preferred_element_type=f32)

    o_ref[...] = jnp.dot(h_ref[...], w_ref[...], preferred_element_type=f32)


def _inproj(x, nw, w_main, w_tail, layer):
    m, d = x.shape
    n = w_main.shape[-1]
    tm = _pick_tile(m, 768)
    tn = DIFF_W
    return pl.pallas_call(
        _inproj_body,
        out_shape=(jax.ShapeDtypeStruct((m, n), f32), jax.ShapeDtypeStruct((m, LANES), f32)),
        grid=(m // tm, n // tn),
        in_specs=[
            pl.BlockSpec((tm, d), lambda i, j: (i, 0)),
            pl.BlockSpec((None, 1, d), lambda i, j: (layer, 0, 0)),
            pl.BlockSpec((None, d, tn), lambda i, j: (layer, 0, j)),
            pl.BlockSpec((None, d, LANES), lambda i, j: (layer, 0, 0)),
        ],
        out_specs=(pl.BlockSpec((tm, tn), lambda i, j: (i, j)),
                   pl.BlockSpec((tm, LANES), lambda i, j: (i, 0))),
        scratch_shapes=[pltpu.VMEM((tm, d), bf16)],
        compiler_params=_params("parallel", "arbitrary"),
        name="inproj",
    )(x, nw, w_main, w_tail)


def _rope_body(dq_ref, dk_ref, dv_ref, c_ref, s1_ref, s2_ref, q_ref, kb_ref, vb_ref, kf_ref):
    c, s1, s2 = c_ref[...], s1_ref[...], s2_ref[...]
    for h in range(DIFF_HEADS):
        sl = slice(h * LANES, (h + 1) * LANES)
        for src, scale, dsts in ((dq_ref, DIFF_DQK ** -0.5, (q_ref,)), (dk_ref, None, (kb_ref, kf_ref))):
            x = src[:, sl]
            half = ROPE_DIM // 2
            r = x * c + pltpu.roll(x, half, 1) * s1 + pltpu.roll(x, LANES - half, 1) * s2
            if scale is not None:
                r = r * scale
            for dst in dsts:
                dst[:, sl] = r.astype(dst.dtype)
    vb_ref[...] = dv_ref[...].astype(bf16)


def _rope(p, tabs):
    m = p.shape[0]
    tr = _pick_tile(m, 384, 16)
    row = lambda i: (i, 0)
    col = lambda cb: pl.BlockSpec((tr, DIFF_W), lambda i: (i, cb))
    tab = pl.BlockSpec((tr, LANES), row)
    return pl.pallas_call(
        _rope_body,
        out_shape=(jax.ShapeDtypeStruct((m, DIFF_W), bf16),) * 3 + (jax.ShapeDtypeStruct((m, DIFF_W), f32),),
        grid=(m // tr,),
        in_specs=[col(COL_DQ), col(COL_DK), col(COL_DV), tab, tab, tab],
        out_specs=(pl.BlockSpec((tr, DIFF_W), row),) * 4,
        compiler_params=_params("parallel"),
        name="rope",
    )(p, p, p, *tabs)


def _rope_tables(pos):
    half = ROPE_DIM // 2
    inv = ROPE_THETA ** (-jnp.arange(half, dtype=f32) / half)
    ang = pos.astype(f32)[:, None] * inv[None, :]
    cos, sin = jnp.cos(ang), jnp.sin(ang)
    n = pos.shape[0]
    rest = DIFF_DQK - ROPE_DIM
    one, zero, zh = jnp.ones((n, rest), f32), jnp.zeros((n, rest), f32), jnp.zeros((n, half), f32)
    c = jnp.concatenate([cos, cos, one], axis=1)
    s1 = jnp.concatenate([zh, sin, zero], axis=1)
    s2 = jnp.concatenate([-sin, zh, zero], axis=1)
    rep = LANES // DIFF_DQK
    return tuple(jnp.tile(t, (1, rep)) for t in (c, s1, s2))


INV_BASE = 16


def _block_masks(row, colm):
    masks = [(row // INV_BASE) == (colm // INV_BASE)]
    nb = INV_BASE
    while nb < row.shape[0]:
        masks.append(((row // nb) == (colm // nb) + 1) & ((colm // nb) % 2 == 0))
        nb *= 2
    return masks


def _tri_inverse(a, eye, masks):
    n = a.shape[0]
    ad = jnp.where(masks[0], a, 0.0)
    t = eye - ad
    p = _dot3(ad, ad)
    steps = int(math.log2(INV_BASE)) - 1
    for s in range(steps):
        if s + 1 < steps:
            tp = _dot3(jnp.concatenate([t, p], axis=0), p)
            t = t + tp[:n]
            p = tp[n:]
        else:
            t = t + _dot3(t, p)
    for m in masks[1:]:
        t = t - _dot3(t, _dot3(jnp.where(m, a, 0.0), t))
    return t


def _gdn_body(qkv_ref, tail_ref, cw_ref, alog_ref, dtb_ref, s0_ref, c0_ref, o_ref, so_ref,
              xp_ref, s_ref, *, n_valid):
    c = pl.program_id(1)
    L = CHUNK

    @pl.when(c == 0)
    def _():
        xp_ref[0:SUBLANES, :] = c0_ref[...]
        s_ref[...] = s0_ref[...]

    xp_ref[SUBLANES:SUBLANES + L, :] = qkv_ref[...]
    cw = cw_ref[...]
    base = SUBLANES - (CONV_W - 1)
    y = xp_ref[base:base + L, :] * cw[0:1, :]
    for i in range(1, CONV_W):
        y = y + xp_ref[base + i:base + i + L, :] * cw[i:i + 1, :]
    xp_ref[0:SUBLANES, :] = xp_ref[L:L + SUBLANES, :]
    y = y * jax.nn.sigmoid(y)

    row = lax.broadcasted_iota(jnp.int32, (L, L), 0)
    colm = lax.broadcasted_iota(jnp.int32, (L, L), 1)
    incl = row >= colm
    strict = row > colm
    eye = jnp.where(row == colm, 1.0, 0.0).astype(f32)
    masks = _block_masks(row, colm)
    ltri = jnp.where(incl, 1.0, 0.0).astype(bf16)
    utri = jnp.where(row <= colm, 1.0, 0.0).astype(bf16)

    t = tail_ref[...]
    valid = (c * L + row) < n_valid
    beta = jnp.where(valid, jax.nn.sigmoid(t), 0.0)
    z = t + dtb_ref[...]
    softplus = jnp.maximum(z, 0.0) + jnp.log(1.0 + jnp.exp(-jnp.abs(z)))
    g = jnp.where(valid, -jnp.exp(alog_ref[...]) * softplus, 0.0)
    g_col = _dot_exact_lhs(ltri, g)
    g_row = _dot_exact_rhs(g.T, utri)
    gam = jnp.exp(g_col)
    g_last = g_col[L - 1:L, :]
    kdec = jnp.exp(g_last - g_col)
    s_scale = jnp.exp(g_last)

    for h in range(GDN_HEADS):
        q = y[:, h * GDN_DK:(h + 1) * GDN_DK]
        k = y[:, GDN_QK_W + h * GDN_DK:GDN_QK_W + (h + 1) * GDN_DK]
        v = y[:, 2 * GDN_QK_W + h * GDN_DV:2 * GDN_QK_W + (h + 1) * GDN_DV]
        q = q * lax.rsqrt(jnp.sum(q * q, axis=-1, keepdims=True) + EPS) * (GDN_DK ** -0.5)
        k = k * lax.rsqrt(jnp.sum(k * k, axis=-1, keepdims=True) + EPS)
        gl = GDN_HEADS + h
        beta_c = beta[:, h:h + 1]
        gam_c = gam[:, gl:gl + 1]
        diff = g_col[:, gl:gl + 1] - g_row[gl:gl + 1, :]
        dec = jnp.exp(jnp.where(incl, diff, NEG))
        s_h = s_ref[h]

        kb = k * beta_c
        k_bf = k.astype(bf16)
        a = _dot_nt(kb, k_bf) * jnp.where(strict, dec, 0.0)
        tinv = _tri_inverse(a, eye, masks)
        rhs = jnp.concatenate([kb * gam_c, v * beta_c], axis=1)
        sol = _dot3(tinv, rhs)
        w, u0 = sol[:, :GDN_DK], sol[:, GDN_DK:]
        u = u0 - _dot(w, s_h)
        qk = _dot_nt(q, k_bf) * dec
        o = _dot(q * gam_c, s_h) + _dot(qk, u)
        o_ref[:, h * GDN_DV:(h + 1) * GDN_DV] = o
        kd = (k * kdec[:, gl:gl + 1]).T
        s_ref[h] = s_scale[:, gl:gl + 1] * s_h + _dot(kd, u)

    @pl.when(c == pl.num_programs(1) - 1)
    def _():
        so_ref[...] = s_ref[...]


def _gdn(qkv_src, tail, conv_w, alog_row, dtb_row, s0, c0, layer, n_seq, n_chunks, n_valid):
    rows = n_seq * n_chunks * CHUNK
    return pl.pallas_call(
        functools.partial(_gdn_body, n_valid=n_valid),
        out_shape=(jax.ShapeDtypeStruct((rows, GDN_V_W), f32),
                   jax.ShapeDtypeStruct((n_seq, GDN_HEADS, GDN_DK, GDN_DV), f32)),
        grid=(n_seq, n_chunks),
        in_specs=[
            pl.BlockSpec((CHUNK, CONV_CH), lambda s, c: (s * n_chunks + c, 0)),
            pl.BlockSpec((CHUNK, LANES), lambda s, c: (s * n_chunks + c, 0)),
            pl.BlockSpec((None, CONV_W, CONV_CH), lambda s, c: (layer, 0, 0)),
            pl.BlockSpec((None, 1, LANES), lambda s, c: (layer, 0, 0)),
            pl.BlockSpec((None, 1, LANES), lambda s, c: (layer, 0, 0)),
            pl.BlockSpec((None, GDN_HEADS, GDN_DK, GDN_DV), lambda s, c: (s, 0, 0, 0)),
            pl.BlockSpec((None, SUBLANES, CONV_CH), lambda s, c: (s, 0, 0)),
        ],
        out_specs=(pl.BlockSpec((CHUNK, GDN_V_W), lambda s, c: (s * n_chunks + c, 0)),
                   pl.BlockSpec((None, GDN_HEADS, GDN_DK, GDN_DV), lambda s, c: (s, 0, 0, 0))),
        scratch_shapes=[pltpu.VMEM((CHUNK + SUBLANES, CONV_CH), f32),
                        pltpu.VMEM((GDN_HEADS, GDN_DK, GDN_DV), f32)],
        compiler_params=_params("parallel", "arbitrary"),
        name="gdn",
    )(qkv_src, tail, conv_w, alog_row, dtb_row, s0, c0)


def _lambda(lv_ref, lam_init):
    lv = lv_ref[...]
    a = jnp.sum(lv[0:1, :] * lv[1:2, :], axis=-1, keepdims=True)
    b = jnp.sum(lv[2:3, :] * lv[3:4, :], axis=-1, keepdims=True)
    return jnp.exp(a) - jnp.exp(b) + lam_init


def _attn_body(lv_ref, q_ref, k_ref, v_ref, o_ref, *, tq, lam_init):
    qi = pl.program_id(2)
    q = q_ref[...].astype(f32)
    lane = lax.broadcasted_iota(jnp.int32, (tq, LANES), 1)
    qs = (jnp.where(lane < DIFF_DQK, q, 0.0).astype(bf16),
          jnp.where(lane >= DIFF_DQK, q, 0.0).astype(bf16))
    row = lax.broadcasted_iota(jnp.int32, (tq, tq), 0)
    colm = lax.broadcasted_iota(jnp.int32, (tq, tq), 1)
    causal = row >= colm

    def step(kj, carry, masked):
        start = pl.multiple_of(kj * tq, tq)
        kb = k_ref[pl.ds(start, tq), :]
        vb = v_ref[pl.ds(start, tq), :]
        out = []
        for c in range(2):
            m, l, acc = carry[3 * c:3 * c + 3]
            s = lax.dot_general(qs[c], kb, (((1,), (1,)), ((), ())), preferred_element_type=f32)
            if masked:
                s = jnp.where(causal, s, NEG)
            mn = jnp.maximum(m, jnp.max(s, axis=-1, keepdims=True))
            alpha = jnp.exp(m - mn)
            p = jnp.exp(s - mn)
            l = alpha * l + jnp.sum(p, axis=-1, keepdims=True)
            acc = alpha * acc + jnp.dot(p.astype(bf16), vb, preferred_element_type=f32)
            out += [mn, l, acc]
        return tuple(out)

    one = (jnp.full((tq, 1), NEG, f32), jnp.zeros((tq, 1), f32), jnp.zeros((tq, DIFF_DV), f32))
    carry = lax.fori_loop(0, qi, lambda kj, cr: step(kj, cr, False), one + one)
    m0, l0, a0, m1, l1, a1 = step(qi, carry, True)
    lam = _lambda(lv_ref, lam_init)
    o_ref[...] = a0 / l0 - lam * (a1 / l1)


def _attn(lam_vecs, q, k, v, layer, lam_init, n_batch, rows_per_seq):
    tq = _pick_tile(rows_per_seq, 384, CHUNK)
    nq = rows_per_seq // tq
    return pl.pallas_call(
        functools.partial(_attn_body, tq=tq, lam_init=lam_init),
        out_shape=jax.ShapeDtypeStruct((n_batch * rows_per_seq, DIFF_HEADS * DIFF_DV), f32),
        grid=(n_batch, DIFF_HEADS, nq),
        in_specs=[
            pl.BlockSpec((None, 4, DIFF_DQK), lambda b, h, i: (layer, 0, 0)),
            pl.BlockSpec((tq, LANES), lambda b, h, i: (b * nq + i, h)),
            pl.BlockSpec((rows_per_seq, LANES), lambda b, h, i: (b, h)),
            pl.BlockSpec((rows_per_seq, LANES), lambda b, h, i: (b, h)),
        ],
        out_specs=pl.BlockSpec((tq, DIFF_DV), lambda b, h, i: (b * nq + i, h)),
        compiler_params=_params("parallel", "parallel", "arbitrary"),
        name="diff_attn",
    )(lam_vecs, q, k, v)


def _decode_body(pt_ref, lv_ref, q_ref, *refs, n_pages_step, lam_init, n_new):
    del pt_ref
    k_refs = refs[:n_pages_step]
    v_refs = refs[n_pages_step:2 * n_pages_step]
    kn_ref, vn_ref, o_ref, m_ref, l_ref, acc_ref = refs[2 * n_pages_step:]
    step = pl.program_id(1)
    n = LANES

    @pl.when(step == 0)
    def _():
        m_ref[...] = jnp.full_like(m_ref, NEG)
        l_ref[...] = jnp.zeros_like(l_ref)
        acc_ref[...] = jnp.zeros_like(acc_ref)

    qbd = q_ref[...]

    def to_rows(x):
        return jnp.broadcast_to(x, (n, n)).T

    def process(ks, vs, mask):
        ss = [jnp.dot(kr[...].astype(bf16), qbd, preferred_element_type=f32) for kr in ks]
        if mask is not None:
            ss = [jnp.where(mask, s, NEG) for s in ss]
        mx = functools.reduce(jnp.maximum, [jnp.max(s, axis=0, keepdims=True) for s in ss])
        m_old = m_ref[...]
        m_new = jnp.maximum(m_old, mx)
        alpha = jnp.exp(m_old - m_new)
        l_new = alpha * l_ref[...]
        alpha_r = to_rows(alpha)
        accs = [acc_ref[h] * alpha_r[h * SUBLANES:(h + 1) * SUBLANES, :] for h in range(DIFF_HEADS)]
        for s, vr in zip(ss, vs):
            p = jnp.exp(s - m_new)
            l_new = l_new + jnp.sum(p, axis=0, keepdims=True)
            pt = p.T
            vb = vr[...].astype(bf16)
            for h in range(DIFF_HEADS):
                accs[h] = accs[h] + jnp.dot(pt[h * SUBLANES:(h + 1) * SUBLANES, :].astype(bf16),
                                            vb[:, h * DIFF_DV:(h + 1) * DIFF_DV],
                                            preferred_element_type=f32)
        m_ref[...] = m_new
        l_ref[...] = l_new
        for h in range(DIFF_HEADS):
            acc_ref[h] = accs[h]

    process(k_refs, v_refs, None)

    @pl.when(step == pl.num_programs(1) - 1)
    def _():
        key = lax.broadcasted_iota(jnp.int32, (n, n), 0)
        tok = lax.broadcasted_iota(jnp.int32, (n, n), 1) % n_new
        process([kn_ref], [vn_ref], key <= tok)
        lam = _lambda(lv_ref, lam_init)
        linv_r = to_rows(1.0 / l_ref[...])
        for h in range(DIFF_HEADS):
            a = acc_ref[h] * linv_r[h * SUBLANES:(h + 1) * SUBLANES, :]
            o_ref[:, h * DIFF_DV:(h + 1) * DIFF_DV] = a - lam * pltpu.roll(a, n_new, 0)


def _decode(page_table, lam_vecs, qbd, cache_k, cache_v, k_new, v_new, layer, lam_init, n_new):
    n_seq, n_pages = page_table.shape
    g = PAGES_PER_STEP
    assert n_pages % g == 0 and 2 * n_new == SUBLANES
    width = cache_k.shape[-1]

    def page_spec(u):
        return pl.BlockSpec((None, None, PAGE_SIZE, width),
                            lambda b, s, pt: (layer, pt[b * n_pages + s * g + u], 0, 0))

    new_spec = pl.BlockSpec((None, PAGE_SIZE, width), lambda b, s, pt: (b, 0, 0))
    grid_spec = pltpu.PrefetchScalarGridSpec(
        num_scalar_prefetch=1,
        grid=(n_seq, n_pages // g),
        in_specs=[pl.BlockSpec((None, 4, DIFF_DQK), lambda b, s, pt: (layer, 0, 0)),
                  pl.BlockSpec((None, width, LANES), lambda b, s, pt: (b, 0, 0))]
                 + [page_spec(u) for u in range(g)] * 2 + [new_spec, new_spec],
        out_specs=pl.BlockSpec((None, SUBLANES, DIFF_HEADS * DIFF_DV), lambda b, s, pt: (b, 0, 0)),
        scratch_shapes=[pltpu.VMEM((1, LANES), f32), pltpu.VMEM((1, LANES), f32),
                        pltpu.VMEM((DIFF_HEADS, SUBLANES, DIFF_DV), f32)],
    )
    return pl.pallas_call(
        functools.partial(_decode_body, n_pages_step=g, lam_init=lam_init, n_new=n_new),
        out_shape=jax.ShapeDtypeStruct((n_seq, SUBLANES, DIFF_HEADS * DIFF_DV), f32),
        grid_spec=grid_spec,
        compiler_params=_params("parallel", "arbitrary"),
        name="decode_attn",
    )(page_table.reshape(-1), lam_vecs, qbd, *([cache_k] * g), *([cache_v] * g), k_new, v_new)


def _merge_body(x_ref, og_ref, z_ref, od_ref, ga_ref, gb_ref, gn_ref, sn_ref, wbg_ref, wbd_ref,
                wo_ref, o_ref, yg_ref, yd_ref, *, d_scale):
    gn, sn = gn_ref[...], sn_ref[...]
    for h in range(GDN_HEADS):
        sl = slice(h * GDN_DV, (h + 1) * GDN_DV)
        zz = z_ref[:, sl]
        yg_ref[:, sl] = (_rms(og_ref[:, sl], gn) * (zz * jax.nn.sigmoid(zz))).astype(bf16)
    for h in range(DIFF_HEADS):
        sl = slice(h * DIFF_DV, (h + 1) * DIFF_DV)
        yd_ref[:, sl] = (_rms(od_ref[:, sl], sn) * d_scale).astype(bf16)
    mix = (jax.nn.sigmoid(ga_ref[...]) * jnp.dot(yg_ref[...], wbg_ref[...], preferred_element_type=f32)
           + jax.nn.sigmoid(gb_ref[...]) * jnp.dot(yd_ref[...], wbd_ref[...], preferred_element_type=f32))
    o_ref[...] = x_ref[...] + jnp.dot(mix.astype(bf16), wo_ref[...], preferred_element_type=f32)


def _merge(x, o_g, p, o_d, gn, sn, wbg, wbd, wo, layer, lam_init):
    m, d = x.shape
    tm = _pick_tile(m, 256)
    row = lambda i: (i, 0)
    wspec = lambda r, c: pl.BlockSpec((None, r, c), lambda i: (layer, 0, 0), pipeline_mode=pl.Buffered(1))
    return pl.pallas_call(
        functools.partial(_merge_body, d_scale=1.0 - lam_init),
        out_shape=jax.ShapeDtypeStruct((m, d), f32),
        grid=(m // tm,),
        in_specs=[
            pl.BlockSpec((tm, d), row),
            pl.BlockSpec((tm, GDN_V_W), row),
            pl.BlockSpec((tm, GDN_V_W), lambda i: (i, COL_Z)),
            pl.BlockSpec((tm, DIFF_W), row),
            pl.BlockSpec((tm, d), lambda i: (i, COL_GA * DIFF_W // d)),
            pl.BlockSpec((tm, d), lambda i: (i, COL_GB * DIFF_W // d)),
            pl.BlockSpec((None, 1, GDN_DV), lambda i: (layer, 0, 0)),
            pl.BlockSpec((None, 1, DIFF_DV), lambda i: (layer, 0, 0)),
            wspec(GDN_V_W, d), wspec(DIFF_W, d), wspec(d, d),
        ],
        out_specs=pl.BlockSpec((tm, d), row),
        scratch_shapes=[pltpu.VMEM((tm, GDN_V_W), bf16), pltpu.VMEM((tm, DIFF_W), bf16)],
        compiler_params=_params("parallel"),
        name="merge",
    )(x, o_g, p, o_d, p, p, gn, sn, wbg, wbd, wo)


def _final_body(x_ref, w_ref, o_ref):
    o_ref[...] = _rms(x_ref[...], w_ref[...])


def _final_norm(x, w):
    m, d = x.shape
    tm = _pick_tile(m, 768)
    return pl.pallas_call(
        _final_body,
        out_shape=jax.ShapeDtypeStruct((m, d), f32),
        grid=(m // tm,),
        in_specs=[pl.BlockSpec((tm, d), lambda i: (i, 0)), pl.BlockSpec((1, d), lambda i: (0, 0))],
        out_specs=pl.BlockSpec((tm, d), lambda i: (i, 0)),
        compiler_params=_params("parallel"),
        name="final_norm",
    )(x, w)


def kernel(x_prompt, x_sample, cache_k, cache_v, state_delta, state_conv, page_table, meta_tokens,
           ffn1_norm, ffn1_gate, ffn1_up, ffn1_down, mix_norm, w_in, conv_w, gdn_a_log, gdn_dt_bias,
           gdn_out_norm, diff_lambda, diff_subln, w_branch_gdn, w_branch_diff, w_out, ffn2_norm,
           ffn2_gate, ffn2_up, ffn2_down, final_norm):
    nb, seq, d = x_prompt.shape
    ns, ls, _ = x_sample.shape
    depth = w_in.shape[0]
    n_tok = seq + N_META
    rps = -(-n_tok // CHUNK) * CHUNK
    n_smp = ns * ls
    assert rps - n_tok >= n_smp and d == 2 * DIFF_W
    past = page_table.shape[1] * PAGE_SIZE
    n_pool = cache_k.shape[1]

    x = jnp.concatenate([jnp.broadcast_to(meta_tokens[None], (nb, N_META, d)), x_prompt,
                         jnp.zeros((nb, rps - n_tok, d), f32)], axis=1).reshape(nb * rps, d)
    s_lo, s_hi = n_tok, n_tok + n_smp
    x = x.at[s_lo:s_hi].set(x_sample.reshape(n_smp, d))

    pos = jnp.tile(jnp.arange(rps, dtype=jnp.int32), nb)
    pos = pos.at[s_lo:s_hi].set(jnp.tile(past + jnp.arange(ls, dtype=jnp.int32), ns))
    tabs = _rope_tables(pos)

    gate_lo = CONV_CH + GDN_V_W
    gate_hi = gate_lo + 2 * GDN_HEADS
    diff_hi = gate_hi + 3 * DIFF_W
    w_main = jnp.concatenate([w_in[:, :, :gate_lo], w_in[:, :, diff_hi:], w_in[:, :, gate_hi:diff_hi]],
                             axis=-1).astype(bf16)
    w_tail = jnp.pad(w_in[:, :, gate_lo:gate_hi], ((0, 0), (0, 0), (0, LANES - 2 * GDN_HEADS))).astype(bf16)
    cast = lambda w: w.astype(bf16)
    f1g, f1u, f1d, f2g, f2u, f2d = map(cast, (ffn1_gate, ffn1_up, ffn1_down, ffn2_gate, ffn2_up, ffn2_down))
    wbg, wbd, wo = map(cast, (w_branch_gdn, w_branch_diff, w_out))
    row3 = lambda w: w.reshape(depth, 1, -1)
    n1, n2, nm, gn, sn = map(row3, (ffn1_norm, ffn2_norm, mix_norm, gdn_out_norm, diff_subln))
    lane_pad = lambda w: jnp.pad(w, ((0, 0), (GDN_HEADS, LANES - 2 * GDN_HEADS))).reshape(depth, 1, LANES)
    alog_row, dtb_row = lane_pad(gdn_a_log), lane_pad(gdn_dt_bias)

    ck = cache_k.reshape(depth, n_pool, PAGE_SIZE, DIFF_W)
    cv = cache_v.reshape(depth, n_pool, PAGE_SIZE, DIFF_W)
    zero_state = jnp.zeros((nb, GDN_HEADS, GDN_DK, GDN_DV), f32)
    zero_conv = jnp.zeros((nb, SUBLANES, CONV_CH), f32)
    conv_pad = jnp.pad(state_conv, ((0, 0), (0, 0), (SUBLANES - (CONV_W - 1), 0), (0, 0)))

    def smp_chunks(a):
        w = a.shape[-1]
        return jnp.pad(a.reshape(ns, ls, w), ((0, 0), (0, CHUNK - ls), (0, 0))).reshape(ns * CHUNK, w)

    outs = [[] for _ in range(8)]
    for l in range(depth):
        lam_init = 0.8 - 0.6 * math.exp(-0.3 * l)
        x = _ffn(x, n1, f1g, f1u, f1d, l)
        p, tail = _inproj(x, nm, w_main, w_tail, l)
        q_bf, k_bf, v_bf, k_f = _rope(p, tabs)

        o_g, st_p = _gdn(p, tail, conv_w, alog_row, dtb_row, zero_state, zero_conv, l, nb, rps // CHUNK, n_tok)
        qkv_s = p[s_lo:s_hi, :CONV_CH]
        og_s, st_s = _gdn(smp_chunks(qkv_s), smp_chunks(tail[s_lo:s_hi]), conv_w, alog_row, dtb_row,
                          state_delta[l], conv_pad[l], l, ns, 1, ls)
        o_g = o_g.at[s_lo:s_hi].set(og_s.reshape(ns, CHUNK, GDN_V_W)[:, :ls].reshape(n_smp, GDN_V_W))

        o_d = _attn(diff_lambda, q_bf, k_bf, v_bf, l, lam_init, nb, rps)
        q_s = q_bf[s_lo:s_hi].reshape(ns, ls, DIFF_HEADS, 2, DIFF_DQK)
        hc = DIFF_HEADS * 2
        eye_hc = jnp.eye(hc, dtype=bf16).reshape(DIFF_HEADS, 2, 1, DIFF_HEADS, 2, 1)
        qbd = (jnp.transpose(q_s, (0, 2, 3, 4, 1))[:, :, :, :, None, None, :]
               * eye_hc[None, :, :, :, :, :, :]).reshape(ns, DIFF_W, hc * ls)
        qbd = jnp.pad(qbd, ((0, 0), (0, 0), (0, LANES - hc * ls)))
        k_new = smp_chunks(k_f[s_lo:s_hi]).reshape(ns, CHUNK, DIFF_W)
        v_s = p[s_lo:s_hi, COL_DV * DIFF_W:(COL_DV + 1) * DIFF_W]
        v_new = smp_chunks(v_s).reshape(ns, CHUNK, DIFF_W)
        od_s = _decode(page_table, diff_lambda, qbd, ck, cv, k_new, v_new, l, lam_init, ls)
        o_d = o_d.at[s_lo:s_hi].set(od_s[:, :ls].reshape(n_smp, DIFF_W))

        x = _merge(x, o_g, p, o_d, gn, sn, wbg, wbd, wo, l, lam_init)
        x = _ffn(x, n2, f2g, f2u, f2d, l)

        seq_view = lambda a: a.reshape(nb, rps, a.shape[-1])
        outs[0].append(seq_view(k_f)[:, :n_tok].reshape(nb, n_tok, DIFF_HEADS, 2, DIFF_DQK))
        outs[1].append(seq_view(p[:, COL_DV * DIFF_W:(COL_DV + 1) * DIFF_W])[:, :n_tok]
                       .reshape(nb, n_tok, DIFF_HEADS, DIFF_DV))
        outs[2].append(st_p)
        outs[3].append(seq_view(p[:, :CONV_CH])[:, n_tok - (CONV_W - 1):n_tok])
        outs[4].append(k_f[s_lo:s_hi].reshape(ns, ls, DIFF_HEADS, 2, DIFF_DQK))
        outs[5].append(v_s.reshape(ns, ls, DIFF_HEADS, DIFF_DV))
        outs[6].append(st_s)
        outs[7].append(jnp.concatenate([state_conv[l], qkv_s.reshape(ns, ls, CONV_CH)], axis=1)[:, -(CONV_W - 1):])

    y = _final_norm(x, final_norm.reshape(1, d))
    y_prompt = y.reshape(nb, rps, d)[:, N_META:n_tok]
    y_sample = y[s_lo:s_hi].reshape(ns, ls, d)
    return (y_prompt, y_sample) + tuple(jnp.stack(o) for o in outs)
```

```python
import functools
import math

import jax
import jax.numpy as jnp
from jax import lax
from jax.experimental import pallas as pl
from jax.experimental.pallas import tpu as pltpu

N_META = 16
GDN_HEADS = 8
GDN_DK = 128
GDN_DV = 128
CONV_W = 4
DIFF_HEADS = 8
DIFF_DQK = 64
DIFF_DV = 128
ROPE_THETA = 500000.0
ROPE_DIM = DIFF_DQK // 4
PAGE_SIZE = 128
EPS = 1e-6

CHUNK = 128
LANES = 128
SUBLANES = 8
NEG = -1e30
VMEM_LIMIT = 56 * 1024 * 1024
PAGES_PER_STEP = 4

GDN_QK_W = GDN_HEADS * GDN_DK
GDN_V_W = GDN_HEADS * GDN_DV
CONV_CH = 2 * GDN_QK_W + GDN_V_W
DIFF_W = DIFF_HEADS * 2 * DIFF_DQK
COL_Z, COL_GA, COL_GB, COL_DQ, COL_DK, COL_DV = 3, 4, 6, 8, 9, 10

f32 = jnp.float32
bf16 = jnp.bfloat16


def _params(*sem):
    return pltpu.CompilerParams(dimension_semantics=sem, vmem_limit_bytes=VMEM_LIMIT)


def _pick_tile(n, target, mult=SUBLANES):
    best = None
    for t in range(mult, min(n, target) + 1, mult):
        if n % t == 0:
            best = t
    assert best is not None, (n, target, mult)
    return best


def _rms(x, w):
    return x * lax.rsqrt(jnp.mean(x * x, axis=-1, keepdims=True) + EPS) * w


def _dot(a, b):
    return jnp.dot(a.astype(bf16), b.astype(bf16), preferred_element_type=f32)


def _dot_nt(a, b):
    return lax.dot_general(a.astype(bf16), b.astype(bf16), (((1,), (1,)), ((), ())),
                           preferred_element_type=f32)


def _split2(x):
    hi = x.astype(bf16)
    lo = (x - hi.astype(f32)).astype(bf16)
    return hi, lo


def _dot3(a, b):
    ah, al = _split2(a)
    bh, bl = _split2(b)
    d = functools.partial(jnp.dot, preferred_element_type=f32)
    return d(ah, bh) + (d(al, bh) + d(ah, bl))


def _dot_exact_lhs(a_bf, x):
    x1 = x.astype(bf16)
    r1 = x - x1.astype(f32)
    x2 = r1.astype(bf16)
    x3 = (r1 - x2.astype(f32)).astype(bf16)
    d = functools.partial(jnp.dot, preferred_element_type=f32)
    return d(a_bf, x1) + (d(a_bf, x2) + d(a_bf, x3))


def _dot_exact_rhs(x, b_bf):
    x1 = x.astype(bf16)
    r1 = x - x1.astype(f32)
    x2 = r1.astype(bf16)
    x3 = (r1 - x2.astype(f32)).astype(bf16)
    d = functools.partial(jnp.dot, preferred_element_type=f32)
    return d(x1, b_bf) + (d(x2, b_bf) + d(x3, b_bf))


def _ffn_body(x_ref, nw_ref, wg_ref, wu_ref, wd_ref, o_ref, h_ref):
    j = pl.program_id(1)

    @pl.when(j == 0)
    def _():
        h_ref[...] = _rms(x_ref[...], nw_ref[...]).astype(bf16)
        o_ref[...] = jnp.zeros_like(o_ref)

    h = h_ref[...]
    g = jnp.dot(h, wg_ref[...], preferred_element_type=f32)
    u = jnp.dot(h, wu_ref[...], preferred_element_type=f32)
    a = (g * jax.nn.sigmoid(g) * u).astype(bf16)
    o_ref[...] += jnp.dot(a, wd_ref[...], preferred_element_type=f32)

    @pl.when(j == pl.num_programs(1) - 1)
    def _():
        o_ref[...] = x_ref[...] + 0.5 * o_ref[...]


def _ffn(x, nw, wg, wu, wd, layer):
    m, d = x.shape
    ff = wg.shape[-1]
    tm = _pick_tile(m, 768)
    tf = _pick_tile(ff, 512, LANES)
    return pl.pallas_call(
        _ffn_body,
        out_shape=jax.ShapeDtypeStruct((m, d), f32),
        grid=(m // tm, ff // tf),
        in_specs=[
            pl.BlockSpec((tm, d), lambda i, j: (i, 0)),
            pl.BlockSpec((None, 1, d), lambda i, j: (layer, 0, 0)),
            pl.BlockSpec((None, d, tf), lambda i, j: (layer, 0, j)),
            pl.BlockSpec((None, d, tf), lambda i, j: (layer, 0, j)),
            pl.BlockSpec((None, tf, d), lambda i, j: (layer, j, 0)),
        ],
        out_specs=pl.BlockSpec((tm, d), lambda i, j: (i, 0)),
        scratch_shapes=[pltpu.VMEM((tm, d), bf16)],
        compiler_params=_params("parallel", "arbitrary"),
        name="ffn",
    )(x, nw, wg, wu, wd)


def _inproj_body(x_ref, nw_ref, w_ref, wt_ref, o_ref, t_ref, h_ref):
    @pl.when(pl.program_id(1) == 0)
    def _():
        h = _rms(x_ref[...], nw_ref[...]).astype(bf16)
        h_ref[...] = h
        t_ref[...] = jnp.dot(h, wt_ref[...], preferred_element_type=f32)

    o_ref[...] = jnp.dot(h_ref[...], w_ref[...], preferred_element_type=f32)


def _inproj(x, nw, w_main, w_tail, layer):
    m, d = x.shape
    n = w_main.shape[-1]
    tm = _pick_tile(m, 768)
    tn = DIFF_W
    return pl.pallas_call(
        _inproj_body,
        out_shape=(jax.ShapeDtypeStruct((m, n), f32), jax.ShapeDtypeStruct((m, LANES), f32)),
        grid=(m // tm, n // tn),
        in_specs=[
            pl.BlockSpec((tm, d), lambda i, j: (i, 0)),
            pl.BlockSpec((None, 1, d), lambda i, j: (layer, 0, 0)),
            pl.BlockSpec((None, d, tn), lambda i, j: (layer, 0, j)),
            pl.BlockSpec((None, d, LANES), lambda i, j: (layer, 0, 0)),
        ],
        out_specs=(pl.BlockSpec((tm, tn), lambda i, j: (i, j)),
                   pl.BlockSpec((tm, LANES), lambda i, j: (i, 0))),
        scratch_shapes=[pltpu.VMEM((tm, d), bf16)],
        compiler_params=_params("parallel", "arbitrary"),
        name="inproj",
    )(x, nw, w_main, w_tail)


def _rope_body(dq_ref, dk_ref, dv_ref, c_ref, s1_ref, s2_ref, q_ref, kb_ref, vb_ref, kf_ref):
    c, s1, s2 = c_ref[...], s1_ref[...], s2_ref[...]
    for h in range(DIFF_HEADS):
        sl = slice(h * LANES, (h + 1) * LANES)
        for src, scale, dsts in ((dq_ref, DIFF_DQK ** -0.5, (q_ref,)), (dk_ref, None, (kb_ref, kf_ref))):
            x = src[:, sl]
            half = ROPE_DIM // 2
            r = x * c + pltpu.roll(x, half, 1) * s1 + pltpu.roll(x, LANES - half, 1) * s2
            if scale is not None:
                r = r * scale
            for dst in dsts:
                dst[:, sl] = r.astype(dst.dtype)
    vb_ref[...] = dv_ref[...].astype(bf16)


def _rope(p, tabs):
    m = p.shape[0]
    tr = _pick_tile(m, 384, 16)
    row = lambda i: (i, 0)
    col = lambda cb: pl.BlockSpec((tr, DIFF_W), lambda i: (i, cb))
    tab = pl.BlockSpec((tr, LANES), row)
    return pl.pallas_call(
        _rope_body,
        out_shape=(jax.ShapeDtypeStruct((m, DIFF_W), bf16),) * 3 + (jax.ShapeDtypeStruct((m, DIFF_W), f32),),
        grid=(m // tr,),
        in_specs=[col(COL_DQ), col(COL_DK), col(COL_DV), tab, tab, tab],
        out_specs=(pl.BlockSpec((tr, DIFF_W), row),) * 4,
        compiler_params=_params("parallel"),
        name="rope",
    )(p, p, p, *tabs)


def _rope_tables(pos):
    half = ROPE_DIM // 2
    inv = ROPE_THETA ** (-jnp.arange(half, dtype=f32) / half)
    ang = pos.astype(f32)[:, None] * inv[None, :]
    cos, sin = jnp.cos(ang), jnp.sin(ang)
    n = pos.shape[0]
    rest = DIFF_DQK - ROPE_DIM
    one, zero, zh = jnp.ones((n, rest), f32), jnp.zeros((n, rest), f32), jnp.zeros((n, half), f32)
    c = jnp.concatenate([cos, cos, one], axis=1)
    s1 = jnp.concatenate([zh, sin, zero], axis=1)
    s2 = jnp.concatenate([-sin, zh, zero], axis=1)
    rep = LANES // DIFF_DQK
    return tuple(jnp.tile(t, (1, rep)) for t in (c, s1, s2))


INV_BASE = 16


def _block_masks(row, colm):
    masks = [(row // INV_BASE) == (colm // INV_BASE)]
    nb = INV_BASE
    while nb < row.shape[0]:
        masks.append(((row // nb) == (colm // nb) + 1) & ((colm // nb) % 2 == 0))
        nb *= 2
    return masks


def _tri_inverse(a_list, eye, masks):
    n = a_list[0].shape[0]
    ad = [jnp.where(masks[0], a, 0.0) for a in a_list]
    t = [eye - x for x in ad]
    p = [_dot3(x, x) for x in ad]
    steps = int(math.log2(INV_BASE)) - 1
    for s in range(steps):
        if s + 1 < steps:
            tp = [_dot3(jnp.concatenate([ti, pi], axis=0), pi) for ti, pi in zip(t, p)]
            t = [ti + x[:n] for ti, x in zip(t, tp)]
            p = [x[n:] for x in tp]
        else:
            t = [ti + _dot3(ti, pi) for ti, pi in zip(t, p)]
    for m in masks[1:]:
        w = [_dot3(jnp.where(m, a, 0.0), ti) for a, ti in zip(a_list, t)]
        t = [ti - _dot3(ti, wi) for ti, wi in zip(t, w)]
    return t


def _gdn_body(qkv_ref, tail_ref, cw_ref, alog_ref, dtb_ref, s0_ref, c0_ref, o_ref, so_ref,
              xp_ref, s_ref, *, n_valid):
    c = pl.program_id(1)
    L = CHUNK

    @pl.when(c == 0)
    def _():
        xp_ref[0:SUBLANES, :] = c0_ref[...]
        s_ref[...] = s0_ref[...]

    xp_ref[SUBLANES:SUBLANES + L, :] = qkv_ref[...]
    cw = cw_ref[...]
    base = SUBLANES - (CONV_W - 1)
    y = xp_ref[base:base + L, :] * cw[0:1, :]
    for i in range(1, CONV_W):
        y = y + xp_ref[base + i:base + i + L, :] * cw[i:i + 1, :]
    xp_ref[0:SUBLANES, :] = xp_ref[L:L + SUBLANES, :]
    y = y * jax.nn.sigmoid(y)

    row = lax.broadcasted_iota(jnp.int32, (L, L), 0)
    colm = lax.broadcasted_iota(jnp.int32, (L, L), 1)
    incl = row >= colm
    strict = row > colm
    eye = jnp.where(row == colm, 1.0, 0.0).astype(f32)
    masks = _block_masks(row, colm)
    ltri = jnp.where(incl, 1.0, 0.0).astype(bf16)
    utri = jnp.where(row <= colm, 1.0, 0.0).astype(bf16)

    t = tail_ref[...]
    valid = (c * L + row) < n_valid
    beta = jnp.where(valid, jax.nn.sigmoid(t), 0.0)
    z = t + dtb_ref[...]
    softplus = jnp.maximum(z, 0.0) + jnp.log(1.0 + jnp.exp(-jnp.abs(z)))
    g = jnp.where(valid, -jnp.exp(alog_ref[...]) * softplus, 0.0)
    g_col = _dot_exact_lhs(ltri, g)
    g_row = _dot_exact_rhs(g.T, utri)
    gam = jnp.exp(g_col)
    g_last = g_col[L - 1:L, :]
    kdec = jnp.exp(g_last - g_col)
    s_scale = jnp.exp(g_last)

    heads = range(GDN_HEADS)
    lane = [GDN_HEADS + h for h in heads]
    l2 = lambda x: x * lax.rsqrt(jnp.sum(x * x, axis=-1, keepdims=True) + EPS)
    q = [l2(y[:, h * GDN_DK:(h + 1) * GDN_DK]) * (GDN_DK ** -0.5) for h in heads]
    k = [l2(y[:, GDN_QK_W + h * GDN_DK:GDN_QK_W + (h + 1) * GDN_DK]) for h in heads]
    v = [y[:, 2 * GDN_QK_W + h * GDN_DV:2 * GDN_QK_W + (h + 1) * GDN_DV] for h in heads]
    beta_c = [beta[:, h:h + 1] for h in heads]
    gam_c = [gam[:, gl:gl + 1] for gl in lane]
    dec = [jnp.exp(jnp.where(incl, g_col[:, gl:gl + 1] - g_row[gl:gl + 1, :], NEG)) for gl in lane]
    kb = [k[h] * beta_c[h] for h in heads]
    k_bf = [x.astype(bf16) for x in k]
    a = [_dot_nt(kb[h], k_bf[h]) * jnp.where(strict, dec[h], 0.0) for h in heads]
    qk = [_dot_nt(q[h], k_bf[h]) * dec[h] for h in heads]
    tinv = _tri_inverse(a, eye, masks)
    sol = [_dot3(tinv[h], jnp.concatenate([kb[h] * gam_c[h], v[h] * beta_c[h]], axis=1)) for h in heads]
    s_old = [s_ref[h] for h in heads]
    u = [sol[h][:, GDN_DK:] - _dot(sol[h][:, :GDN_DK], s_old[h]) for h in heads]
    o = [_dot(q[h] * gam_c[h], s_old[h]) + _dot(qk[h], u[h]) for h in heads]
    for h in heads:
        o_ref[:, h * GDN_DV:(h + 1) * GDN_DV] = o[h]
    for h, gl in zip(heads, lane):
        kd = (k[h] * kdec[:, gl:gl + 1]).T
        s_ref[h] = s_scale[:, gl:gl + 1] * s_old[h] + _dot(kd, u[h])

    @pl.when(c == pl.num_programs(1) - 1)
    def _():
        so_ref[...] = s_ref[...]


def _gdn(qkv_src, tail, conv_w, alog_row, dtb_row, s0, c0, layer, n_seq, n_chunks, n_valid):
    rows = n_seq * n_chunks * CHUNK
    return pl.pallas_call(
        functools.partial(_gdn_body, n_valid=n_valid),
        out_shape=(jax.ShapeDtypeStruct((rows, GDN_V_W), f32),
                   jax.ShapeDtypeStruct((n_seq, GDN_HEADS, GDN_DK, GDN_DV), f32)),
        grid=(n_seq, n_chunks),
        in_specs=[
            pl.BlockSpec((CHUNK, CONV_CH), lambda s, c: (s * n_chunks + c, 0)),
            pl.BlockSpec((CHUNK, LANES), lambda s, c: (s * n_chunks + c, 0)),
            pl.BlockSpec((None, CONV_W, CONV_CH), lambda s, c: (layer, 0, 0)),
            pl.BlockSpec((None, 1, LANES), lambda s, c: (layer, 0, 0)),
            pl.BlockSpec((None, 1, LANES), lambda s, c: (layer, 0, 0)),
            pl.BlockSpec((None, GDN_HEADS, GDN_DK, GDN_DV), lambda s, c: (s, 0, 0, 0)),
            pl.BlockSpec((None, SUBLANES, CONV_CH), lambda s, c: (s, 0, 0)),
        ],
        out_specs=(pl.BlockSpec((CHUNK, GDN_V_W), lambda s, c: (s * n_chunks + c, 0)),
                   pl.BlockSpec((None, GDN_HEADS, GDN_DK, GDN_DV), lambda s, c: (s, 0, 0, 0))),
        scratch_shapes=[pltpu.VMEM((CHUNK + SUBLANES, CONV_CH), f32),
                        pltpu.VMEM((GDN_HEADS, GDN_DK, GDN_DV), f32)],
        compiler_params=_params("parallel", "arbitrary"),
        name="gdn",
    )(qkv_src, tail, conv_w, alog_row, dtb_row, s0, c0)


def _lambda(lv_ref, lam_init):
    lv = lv_ref[...]
    a = jnp.sum(lv[0:1, :] * lv[1:2, :], axis=-1, keepdims=True)
    b = jnp.sum(lv[2:3, :] * lv[3:4, :], axis=-1, keepdims=True)
    return jnp.exp(a) - jnp.exp(b) + lam_init


def _attn_body(lv_ref, q_ref, k_ref, v_ref, o_ref, *, tq, lam_init):
    qi = pl.program_id(2)
    q = q_ref[...].astype(f32)
    lane = lax.broadcasted_iota(jnp.int32, (tq, LANES), 1)
    qs = (jnp.where(lane < DIFF_DQK, q, 0.0).astype(bf16),
          jnp.where(lane >= DIFF_DQK, q, 0.0).astype(bf16))
    row = lax.broadcasted_iota(jnp.int32, (tq, tq), 0)
    colm = lax.broadcasted_iota(jnp.int32, (tq, tq), 1)
    causal = row >= colm

    def step(kj, carry, masked):
        start = pl.multiple_of(kj * tq, tq)
        kb = k_ref[pl.ds(start, tq), :]
        vb = v_ref[pl.ds(start, tq), :]
        out = []
        ss = [lax.dot_general(qc, kb, (((1,), (1,)), ((), ())), preferred_element_type=f32) for qc in qs]
        for c in range(2):
            m, l, acc = carry[3 * c:3 * c + 3]
            s = ss[c]
            if masked:
                s = jnp.where(causal, s, NEG)
            mn = jnp.maximum(m, jnp.max(s, axis=-1, keepdims=True))
            alpha = jnp.exp(m - mn)
            p = jnp.exp(s - mn)
            l = alpha * l + jnp.sum(p, axis=-1, keepdims=True)
            acc = alpha * acc + jnp.dot(p.astype(bf16), vb, preferred_element_type=f32)
            out += [mn, l, acc]
        return tuple(out)

    one = (jnp.full((tq, 1), NEG, f32), jnp.zeros((tq, 1), f32), jnp.zeros((tq, DIFF_DV), f32))
    carry = lax.fori_loop(0, qi, lambda kj, cr: step(kj, cr, False), one + one)
    m0, l0, a0, m1, l1, a1 = step(qi, carry, True)
    lam = _lambda(lv_ref, lam_init)
    o_ref[...] = a0 / l0 - lam * (a1 / l1)


def _attn(lam_vecs, q, k, v, layer, lam_init, n_batch, rows_per_seq):
    tq = _pick_tile(rows_per_seq, 384, CHUNK)
    nq = rows_per_seq // tq
    return pl.pallas_call(
        functools.partial(_attn_body, tq=tq, lam_init=lam_init),
        out_shape=jax.ShapeDtypeStruct((n_batch * rows_per_seq, DIFF_HEADS * DIFF_DV), f32),
        grid=(n_batch, DIFF_HEADS, nq),
        in_specs=[
            pl.BlockSpec((None, 4, DIFF_DQK), lambda b, h, i: (layer, 0, 0)),
            pl.BlockSpec((tq, LANES), lambda b, h, i: (b * nq + i, h)),
            pl.BlockSpec((rows_per_seq, LANES), lambda b, h, i: (b, h)),
            pl.BlockSpec((rows_per_seq, LANES), lambda b, h, i: (b, h)),
        ],
        out_specs=pl.BlockSpec((tq, DIFF_DV), lambda b, h, i: (b * nq + i, h)),
        compiler_params=_params("parallel", "parallel", "arbitrary"),
        name="diff_attn",
    )(lam_vecs, q, k, v)


def _decode_body(pt_ref, lv_ref, q_ref, *refs, n_pages_step, lam_init, n_new):
    del pt_ref
    k_refs = refs[:n_pages_step]
    v_refs = refs[n_pages_step:2 * n_pages_step]
    kn_ref, vn_ref, o_ref, m_ref, l_ref, acc_ref = refs[2 * n_pages_step:]
    step = pl.program_id(1)
    nq = q_ref.shape[0]

    @pl.when(step == 0)
    def _():
        m_ref[...] = jnp.full_like(m_ref, NEG)
        l_ref[...] = jnp.zeros_like(l_ref)
        acc_ref[...] = jnp.zeros_like(acc_ref)

    qt = q_ref[...]

    def process(ks, vs, mask):
        ss = [jnp.dot(qt, kr[...].astype(bf16), preferred_element_type=f32) for kr in ks]
        if mask is not None:
            ss = [jnp.where(mask, s, NEG) for s in ss]
        mx = functools.reduce(jnp.maximum, [jnp.max(s, axis=1, keepdims=True) for s in ss])
        m_old = m_ref[...]
        m_new = jnp.maximum(m_old, mx)
        alpha = jnp.exp(m_old - m_new)
        ps = [jnp.exp(s - m_new) for s in ss]
        l_ref[...] = alpha * l_ref[...] + functools.reduce(
            jnp.add, [jnp.sum(p, axis=1, keepdims=True) for p in ps])
        m_ref[...] = m_new
        for h in range(DIFF_HEADS):
            rows = slice(h * SUBLANES, (h + 1) * SUBLANES)
            acc = acc_ref[h] * alpha[rows]
            for p, vr in zip(ps, vs):
                vh = vr[pl.ds(h, PAGE_SIZE, stride=DIFF_HEADS), :].astype(bf16)
                acc = acc + jnp.dot(p[rows].astype(bf16), vh, preferred_element_type=f32)
            acc_ref[h] = acc

    process(k_refs, v_refs, None)

    @pl.when(step == pl.num_programs(1) - 1)
    def _():
        tok = lax.broadcasted_iota(jnp.int32, (nq, PAGE_SIZE), 0) % n_new
        key = lax.broadcasted_iota(jnp.int32, (nq, PAGE_SIZE), 1)
        process([kn_ref], [vn_ref], key <= tok)
        lam = _lambda(lv_ref, lam_init)
        linv = 1.0 / l_ref[...]
        for h in range(DIFF_HEADS):
            a = acc_ref[h] * linv[h * SUBLANES:(h + 1) * SUBLANES]
            o_ref[:, h * DIFF_DV:(h + 1) * DIFF_DV] = a - lam * pltpu.roll(a, n_new, 0)


def _decode(page_table, lam_vecs, qt, cache_kt, cache_v, kt_new, v_new, layer, lam_init, n_new):
    n_seq, n_pages = page_table.shape
    g = PAGES_PER_STEP
    nq = qt.shape[1]
    assert n_pages % g == 0 and 2 * n_new == SUBLANES and nq == DIFF_HEADS * SUBLANES
    kr, vr = cache_kt.shape[2], cache_v.shape[2]

    def page_spec(rows, u):
        return pl.BlockSpec((None, None, rows, LANES),
                            lambda b, s, pt: (layer, pt[b * n_pages + s * g + u], 0, 0))

    new_spec = lambda rows: pl.BlockSpec((None, rows, LANES), lambda b, s, pt: (b, 0, 0))
    grid_spec = pltpu.PrefetchScalarGridSpec(
        num_scalar_prefetch=1,
        grid=(n_seq, n_pages // g),
        in_specs=[pl.BlockSpec((None, 4, DIFF_DQK), lambda b, s, pt: (layer, 0, 0)),
                  pl.BlockSpec((None, nq, kr), lambda b, s, pt: (b, 0, 0))]
                 + [page_spec(kr, u) for u in range(g)] + [page_spec(vr, u) for u in range(g)]
                 + [new_spec(kr), new_spec(vr)],
        out_specs=pl.BlockSpec((None, SUBLANES, DIFF_HEADS * DIFF_DV), lambda b, s, pt: (b, 0, 0)),
        scratch_shapes=[pltpu.VMEM((nq, 1), f32), pltpu.VMEM((nq, 1), f32),
                        pltpu.VMEM((DIFF_HEADS, SUBLANES, DIFF_DV), f32)],
    )
    return pl.pallas_call(
        functools.partial(_decode_body, n_pages_step=g, lam_init=lam_init, n_new=n_new),
        out_shape=jax.ShapeDtypeStruct((n_seq, SUBLANES, DIFF_HEADS * DIFF_DV), f32),
        grid_spec=grid_spec,
        compiler_params=_params("parallel", "arbitrary"),
        name="decode_attn",
    )(page_table.reshape(-1), lam_vecs, qt, *([cache_kt] * g), *([cache_v] * g), kt_new, v_new)


def _merge_body(x_ref, og_ref, z_ref, od_ref, ga_ref, gb_ref, gn_ref, sn_ref, wbg_ref, wbd_ref,
                wo_ref, o_ref, yg_ref, yd_ref, *, d_scale):
    gn, sn = gn_ref[...], sn_ref[...]
    for h in range(GDN_HEADS):
        sl = slice(h * GDN_DV, (h + 1) * GDN_DV)
        zz = z_ref[:, sl]
        yg_ref[:, sl] = (_rms(og_ref[:, sl], gn) * (zz * jax.nn.sigmoid(zz))).astype(bf16)
    for h in range(DIFF_HEADS):
        sl = slice(h * DIFF_DV, (h + 1) * DIFF_DV)
        yd_ref[:, sl] = (_rms(od_ref[:, sl], sn) * d_scale).astype(bf16)
    mix = (jax.nn.sigmoid(ga_ref[...]) * jnp.dot(yg_ref[...], wbg_ref[...], preferred_element_type=f32)
           + jax.nn.sigmoid(gb_ref[...]) * jnp.dot(yd_ref[...], wbd_ref[...], preferred_element_type=f32))
    o_ref[...] = x_ref[...] + jnp.dot(mix.astype(bf16), wo_ref[...], preferred_element_type=f32)


def _merge(x, o_g, p, o_d, gn, sn, wbg, wbd, wo, layer, lam_init):
    m, d = x.shape
    tm = _pick_tile(m, 256)
    row = lambda i: (i, 0)
    wspec = lambda r, c: pl.BlockSpec((None, r, c), lambda i: (layer, 0, 0), pipeline_mode=pl.Buffered(1))
    return pl.pallas_call(
        functools.partial(_merge_body, d_scale=1.0 - lam_init),
        out_shape=jax.ShapeDtypeStruct((m, d), f32),
        grid=(m // tm,),
        in_specs=[
            pl.BlockSpec((tm, d), row),
            pl.BlockSpec((tm, GDN_V_W), row),
            pl.BlockSpec((tm, GDN_V_W), lambda i: (i, COL_Z)),
            pl.BlockSpec((tm, DIFF_W), row),
            pl.BlockSpec((tm, d), lambda i: (i, COL_GA * DIFF_W // d)),
            pl.BlockSpec((tm, d), lambda i: (i, COL_GB * DIFF_W // d)),
            pl.BlockSpec((None, 1, GDN_DV), lambda i: (layer, 0, 0)),
            pl.BlockSpec((None, 1, DIFF_DV), lambda i: (layer, 0, 0)),
            wspec(GDN_V_W, d), wspec(DIFF_W, d), wspec(d, d),
        ],
        out_specs=pl.BlockSpec((tm, d), row),
        scratch_shapes=[pltpu.VMEM((tm, GDN_V_W), bf16), pltpu.VMEM((tm, DIFF_W), bf16)],
        compiler_params=_params("parallel"),
        name="merge",
    )(x, o_g, p, o_d, p, p, gn, sn, wbg, wbd, wo)


def _final_body(x_ref, w_ref, o_ref):
    o_ref[...] = _rms(x_ref[...], w_ref[...])


def _final_norm(x, w):
    m, d = x.shape
    tm = _pick_tile(m, 768)
    return pl.pallas_call(
        _final_body,
        out_shape=jax.ShapeDtypeStruct((m, d), f32),
        grid=(m // tm,),
        in_specs=[pl.BlockSpec((tm, d), lambda i: (i, 0)), pl.BlockSpec((1, d), lambda i: (0, 0))],
        out_specs=pl.BlockSpec((tm, d), lambda i: (i, 0)),
        compiler_params=_params("parallel"),
        name="final_norm",
    )(x, w)


def kernel(x_prompt, x_sample, cache_k, cache_v, state_delta, state_conv, page_table, meta_tokens,
           ffn1_norm, ffn1_gate, ffn1_up, ffn1_down, mix_norm, w_in, conv_w, gdn_a_log, gdn_dt_bias,
           gdn_out_norm, diff_lambda, diff_subln, w_branch_gdn, w_branch_diff, w_out, ffn2_norm,
           ffn2_gate, ffn2_up, ffn2_down, final_norm):
    nb, seq, d = x_prompt.shape
    ns, ls, _ = x_sample.shape
    depth = w_in.shape[0]
    n_tok = seq + N_META
    rps = -(-n_tok // CHUNK) * CHUNK
    n_smp = ns * ls
    assert rps - n_tok >= n_smp and d == 2 * DIFF_W
    past = page_table.shape[1] * PAGE_SIZE
    n_pool = cache_k.shape[1]

    x = jnp.concatenate([jnp.broadcast_to(meta_tokens[None], (nb, N_META, d)), x_prompt,
                         jnp.zeros((nb, rps - n_tok, d), f32)], axis=1).reshape(nb * rps, d)
    s_lo, s_hi = n_tok, n_tok + n_smp
    x = x.at[s_lo:s_hi].set(x_sample.reshape(n_smp, d))

    pos = jnp.tile(jnp.arange(rps, dtype=jnp.int32), nb)
    pos = pos.at[s_lo:s_hi].set(jnp.tile(past + jnp.arange(ls, dtype=jnp.int32), ns))
    tabs = _rope_tables(pos)

    gate_lo = CONV_CH + GDN_V_W
    gate_hi = gate_lo + 2 * GDN_HEADS
    diff_hi = gate_hi + 3 * DIFF_W
    w_main = jnp.concatenate([w_in[:, :, :gate_lo], w_in[:, :, diff_hi:], w_in[:, :, gate_hi:diff_hi]],
                             axis=-1).astype(bf16)
    w_tail = jnp.pad(w_in[:, :, gate_lo:gate_hi], ((0, 0), (0, 0), (0, LANES - 2 * GDN_HEADS))).astype(bf16)
    cast = lambda w: w.astype(bf16)
    f1g, f1u, f1d, f2g, f2u, f2d = map(cast, (ffn1_gate, ffn1_up, ffn1_down, ffn2_gate, ffn2_up, ffn2_down))
    wbg, wbd, wo = map(cast, (w_branch_gdn, w_branch_diff, w_out))
    row3 = lambda w: w.reshape(depth, 1, -1)
    n1, n2, nm, gn, sn = map(row3, (ffn1_norm, ffn2_norm, mix_norm, gdn_out_norm, diff_subln))
    lane_pad = lambda w: jnp.pad(w, ((0, 0), (GDN_HEADS, LANES - 2 * GDN_HEADS))).reshape(depth, 1, LANES)
    alog_row, dtb_row = lane_pad(gdn_a_log), lane_pad(gdn_dt_bias)

    ck = jnp.transpose(cache_k, (0, 1, 3, 4, 5, 2)).reshape(depth, n_pool, DIFF_W, PAGE_SIZE)
    cv = cache_v.reshape(depth, n_pool, PAGE_SIZE * DIFF_HEADS, DIFF_DV)
    zero_state = jnp.zeros((nb, GDN_HEADS, GDN_DK, GDN_DV), f32)
    zero_conv = jnp.zeros((nb, SUBLANES, CONV_CH), f32)
    conv_pad = jnp.pad(state_conv, ((0, 0), (0, 0), (SUBLANES - (CONV_W - 1), 0), (0, 0)))

    def smp_chunks(a):
        w = a.shape[-1]
        return jnp.pad(a.reshape(ns, ls, w), ((0, 0), (0, CHUNK - ls), (0, 0))).reshape(ns * CHUNK, w)

    outs = [[] for _ in range(8)]
    for l in range(depth):
        lam_init = 0.8 - 0.6 * math.exp(-0.3 * l)
        x = _ffn(x, n1, f1g, f1u, f1d, l)
        p, tail = _inproj(x, nm, w_main, w_tail, l)
        q_bf, k_bf, v_bf, k_f = _rope(p, tabs)

        o_g, st_p = _gdn(p, tail, conv_w, alog_row, dtb_row, zero_state, zero_conv, l, nb, rps // CHUNK, n_tok)
        qkv_s = p[s_lo:s_hi, :CONV_CH]
        og_s, st_s = _gdn(smp_chunks(qkv_s), smp_chunks(tail[s_lo:s_hi]), conv_w, alog_row, dtb_row,
                          state_delta[l], conv_pad[l], l, ns, 1, ls)
        o_g = o_g.at[s_lo:s_hi].set(og_s.reshape(ns, CHUNK, GDN_V_W)[:, :ls].reshape(n_smp, GDN_V_W))

        o_d = _attn(diff_lambda, q_bf, k_bf, v_bf, l, lam_init, nb, rps)
        q_s = q_bf[s_lo:s_hi].reshape(ns, ls, DIFF_HEADS, 2, DIFF_DQK)
        hc = DIFF_HEADS * 2
        eye_hc = jnp.eye(hc, dtype=bf16).reshape(DIFF_HEADS, 2, 1, DIFF_HEADS, 2, 1)
        qt = (jnp.transpose(q_s, (0, 2, 3, 1, 4))[:, :, :, :, None, None, :]
              * eye_hc[None, :, :, :, :, :, :]).reshape(ns, hc * ls, DIFF_W)
        kt_new = jnp.pad(jnp.transpose(k_f[s_lo:s_hi].reshape(ns, ls, DIFF_W), (0, 2, 1)),
                         ((0, 0), (0, 0), (0, PAGE_SIZE - ls)))
        v_s = p[s_lo:s_hi, COL_DV * DIFF_W:(COL_DV + 1) * DIFF_W]
        v_new = jnp.pad(v_s.reshape(ns, ls, DIFF_HEADS, DIFF_DV),
                        ((0, 0), (0, PAGE_SIZE - ls), (0, 0), (0, 0))).reshape(ns, PAGE_SIZE * DIFF_HEADS, DIFF_DV)
        od_s = _decode(page_table, diff_lambda, qt, ck, cv, kt_new, v_new, l, lam_init, ls)
        o_d = o_d.at[s_lo:s_hi].set(od_s[:, :ls].reshape(n_smp, DIFF_W))

        x = _merge(x, o_g, p, o_d, gn, sn, wbg, wbd, wo, l, lam_init)
        x = _ffn(x, n2, f2g, f2u, f2d, l)

        seq_view = lambda a: a.reshape(nb, rps, a.shape[-1])
        outs[0].append(seq_view(k_f)[:, :n_tok].reshape(nb, n_tok, DIFF_HEADS, 2, DIFF_DQK))
        outs[1].append(seq_view(p[:, COL_DV * DIFF_W:(COL_DV + 1) * DIFF_W])[:, :n_tok]
                       .reshape(nb, n_tok, DIFF_HEADS, DIFF_DV))
        outs[2].append(st_p)
        outs[3].append(seq_view(p[:, :CONV_CH])[:, n_tok - (CONV_W - 1):n_tok])
        outs[4].append(k_f[s_lo:s_hi].reshape(ns, ls, DIFF_HEADS, 2, DIFF_DQK))
        outs[5].append(v_s.reshape(ns, ls, DIFF_HEADS, DIFF_DV))
        outs[6].append(st_s)
        outs[7].append(jnp.concatenate([state_conv[l], qkv_s.reshape(ns, ls, CONV_CH)], axis=1)[:, -(CONV_W - 1):])

    y = _final_norm(x, final_norm.reshape(1, d))
    y_prompt = y.reshape(nb, rps, d)[:, N_META:n_tok]
    y_sample = y[s_lo:s_hi].reshape(ns, ls, d)
    return (y_prompt, y_sample) + tuple(jnp.stack(o) for o in outs)
```

```python
import functools
import math

import jax
import jax.numpy as jnp
from jax import lax
from jax.experimental import pallas as pl
from jax.experimental.pallas import tpu as pltpu

N_META = 16
GDN_HEADS = 8
GDN_DK = 128
GDN_DV = 128
CONV_W = 4
DIFF_HEADS = 8
DIFF_DQK = 64
DIFF_DV = 128
ROPE_THETA = 500000.0
ROPE_DIM = DIFF_DQK // 4
PAGE_SIZE = 128
EPS = 1e-6

CHUNK = 128
LANES = 128
SUBLANES = 8
NEG = -1e30
VMEM_LIMIT = 56 * 1024 * 1024
PAGES_PER_STEP = 4

GDN_QK_W = GDN_HEADS * GDN_DK
GDN_V_W = GDN_HEADS * GDN_DV
CONV_CH = 2 * GDN_QK_W + GDN_V_W
DIFF_W = DIFF_HEADS * 2 * DIFF_DQK
COL_Z, COL_GA, COL_GB, COL_DQ, COL_DK, COL_DV = 3, 4, 6, 8, 9, 10

f32 = jnp.float32
bf16 = jnp.bfloat16


def _params(*sem):
    return pltpu.CompilerParams(dimension_semantics=sem, vmem_limit_bytes=VMEM_LIMIT)


def _pick_tile(n, target, mult=SUBLANES):
    best = None
    for t in range(mult, min(n, target) + 1, mult):
        if n % t == 0:
            best = t
    assert best is not None, (n, target, mult)
    return best


def _rms(x, w):
    return x * lax.rsqrt(jnp.mean(x * x, axis=-1, keepdims=True) + EPS) * w


def _dot(a, b):
    return jnp.dot(a.astype(bf16), b.astype(bf16), preferred_element_type=f32)


def _dot_nt(a, b):
    return lax.dot_general(a.astype(bf16), b.astype(bf16), (((1,), (1,)), ((), ())),
                           preferred_element_type=f32)


def _split2(x):
    hi = x.astype(bf16)
    lo = (x - hi.astype(f32)).astype(bf16)
    return hi, lo


def _dot3(a, b):
    ah, al = _split2(a)
    bh, bl = _split2(b)
    d = functools.partial(jnp.dot, preferred_element_type=f32)
    return d(ah, bh) + (d(al, bh) + d(ah, bl))


def _dot_exact_lhs(a_bf, x):
    x1 = x.astype(bf16)
    r1 = x - x1.astype(f32)
    x2 = r1.astype(bf16)
    x3 = (r1 - x2.astype(f32)).astype(bf16)
    d = functools.partial(jnp.dot, preferred_element_type=f32)
    return d(a_bf, x1) + (d(a_bf, x2) + d(a_bf, x3))


def _dot_exact_rhs(x, b_bf):
    x1 = x.astype(bf16)
    r1 = x - x1.astype(f32)
    x2 = r1.astype(bf16)
    x3 = (r1 - x2.astype(f32)).astype(bf16)
    d = functools.partial(jnp.dot, preferred_element_type=f32)
    return d(x1, b_bf) + (d(x2, b_bf) + d(x3, b_bf))


def _ffn_body(x_ref, nw_ref, wg_ref, wu_ref, wd_ref, o_ref, h_ref):
    j = pl.program_id(1)

    @pl.when(j == 0)
    def _():
        h_ref[...] = _rms(x_ref[...], nw_ref[...]).astype(bf16)
        o_ref[...] = jnp.zeros_like(o_ref)

    h = h_ref[...]
    g = jnp.dot(h, wg_ref[...], preferred_element_type=f32)
    u = jnp.dot(h, wu_ref[...], preferred_element_type=f32)
    a = (g * jax.nn.sigmoid(g) * u).astype(bf16)
    o_ref[...] += jnp.dot(a, wd_ref[...], preferred_element_type=f32)

    @pl.when(j == pl.num_programs(1) - 1)
    def _():
        o_ref[...] = x_ref[...] + 0.5 * o_ref[...]


def _ffn(x, nw, wg, wu, wd, layer):
    m, d = x.shape
    ff = wg.shape[-1]
    tm = _pick_tile(m, 768)
    tf = _pick_tile(ff, 512, LANES)
    return pl.pallas_call(
        _ffn_body,
        out_shape=jax.ShapeDtypeStruct((m, d), f32),
        grid=(m // tm, ff // tf),
        in_specs=[
            pl.BlockSpec((tm, d), lambda i, j: (i, 0)),
            pl.BlockSpec((None, 1, d), lambda i, j: (layer, 0, 0)),
            pl.BlockSpec((None, d, tf), lambda i, j: (layer, 0, j)),
            pl.BlockSpec((None, d, tf), lambda i, j: (layer, 0, j)),
            pl.BlockSpec((None, tf, d), lambda i, j: (layer, j, 0)),
        ],
        out_specs=pl.BlockSpec((tm, d), lambda i, j: (i, 0)),
        scratch_shapes=[pltpu.VMEM((tm, d), bf16)],
        compiler_params=_params("parallel", "arbitrary"),
        name="ffn",
    )(x, nw, wg, wu, wd)


def _inproj_body(x_ref, nw_ref, w_ref, wt_ref, o_ref, t_ref, h_ref):
    @pl.when(pl.program_id(1) == 0)
    def _():
        h = _rms(x_ref[...], nw_ref[...]).astype(bf16)
        h_ref[...] = h
        t_ref[...] = jnp.dot(h, wt_ref[...], preferred_element_type=f32)

    o_ref[...] = jnp.dot(h_ref[...], w_ref[...], preferred_element_type=f32)


def _inproj(x, nw, w_main, w_tail, layer):
    m, d = x.shape
    n = w_main.shape[-1]
    tm = _pick_tile(m, 768)
    tn = DIFF_W
    return pl.pallas_call(
        _inproj_body,
        out_shape=(jax.ShapeDtypeStruct((m, n), f32), jax.ShapeDtypeStruct((m, LANES), f32)),
        grid=(m // tm, n // tn),
        in_specs=[
            pl.BlockSpec((tm, d), lambda i, j: (i, 0)),
            pl.BlockSpec((None, 1, d), lambda i, j: (layer, 0, 0)),
            pl.BlockSpec((None, d, tn), lambda i, j: (layer, 0, j)),
            pl.BlockSpec((None, d, LANES), lambda i, j: (layer, 0, 0)),
        ],
        out_specs=(pl.BlockSpec((tm, tn), lambda i, j: (i, j)),
                   pl.BlockSpec((tm, LANES), lambda i, j: (i, 0))),
        scratch_shapes=[pltpu.VMEM((tm, d), bf16)],
        compiler_params=_params("parallel", "arbitrary"),
        name="inproj",
    )(x, nw, w_main, w_tail)


def _rope_body(dq_ref, dk_ref, dv_ref, c_ref, s1_ref, s2_ref, q_ref, kb_ref, vb_ref, kf_ref):
    c, s1, s2 = c_ref[...], s1_ref[...], s2_ref[...]
    for h in range(DIFF_HEADS):
        sl = slice(h * LANES, (h + 1) * LANES)
        for src, scale, dsts in ((dq_ref, DIFF_DQK ** -0.5, (q_ref,)), (dk_ref, None, (kb_ref, kf_ref))):
            x = src[:, sl]
            half = ROPE_DIM // 2
            r = x * c + pltpu.roll(x, half, 1) * s1 + pltpu.roll(x, LANES - half, 1) * s2
            if scale is not None:
                r = r * scale
            for dst in dsts:
                dst[:, sl] = r.astype(dst.dtype)
    vb_ref[...] = dv_ref[...].astype(bf16)


def _rope(p, tabs):
    m = p.shape[0]
    tr = _pick_tile(m, 384, 16)
    row = lambda i: (i, 0)
    col = lambda cb: pl.BlockSpec((tr, DIFF_W), lambda i: (i, cb))
    tab = pl.BlockSpec((tr, LANES), row)
    return pl.pallas_call(
        _rope_body,
        out_shape=(jax.ShapeDtypeStruct((m, DIFF_W), bf16),) * 3 + (jax.ShapeDtypeStruct((m, DIFF_W), f32),),
        grid=(m // tr,),
        in_specs=[col(COL_DQ), col(COL_DK), col(COL_DV), tab, tab, tab],
        out_specs=(pl.BlockSpec((tr, DIFF_W), row),) * 4,
        compiler_params=_params("parallel"),
        name="rope",
    )(p, p, p, *tabs)


def _rope_tables(pos):
    half = ROPE_DIM // 2
    inv = ROPE_THETA ** (-jnp.arange(half, dtype=f32) / half)
    ang = pos.astype(f32)[:, None] * inv[None, :]
    cos, sin = jnp.cos(ang), jnp.sin(ang)
    n = pos.shape[0]
    rest = DIFF_DQK - ROPE_DIM
    one, zero, zh = jnp.ones((n, rest), f32), jnp.zeros((n, rest), f32), jnp.zeros((n, half), f32)
    c = jnp.concatenate([cos, cos, one], axis=1)
    s1 = jnp.concatenate([zh, sin, zero], axis=1)
    s2 = jnp.concatenate([-sin, zh, zero], axis=1)
    rep = LANES // DIFF_DQK
    return tuple(jnp.tile(t, (1, rep)) for t in (c, s1, s2))


INV_BASE = 16


def _block_masks(row, colm):
    masks = [(row // INV_BASE) == (colm // INV_BASE)]
    nb = INV_BASE
    while nb < row.shape[0]:
        masks.append(((row // nb) == (colm // nb) + 1) & ((colm // nb) % 2 == 0))
        nb *= 2
    return masks


def _tri_inverse(a_list, eye, masks):
    n = a_list[0].shape[0]
    ad = [jnp.where(masks[0], a, 0.0) for a in a_list]
    t = [eye - x for x in ad]
    p = [_dot3(x, x) for x in ad]
    steps = int(math.log2(INV_BASE)) - 1
    for s in range(steps):
        if s + 1 < steps:
            tp = [_dot3(jnp.concatenate([ti, pi], axis=0), pi) for ti, pi in zip(t, p)]
            t = [ti + x[:n] for ti, x in zip(t, tp)]
            p = [x[n:] for x in tp]
        else:
            t = [ti + _dot3(ti, pi) for ti, pi in zip(t, p)]
    for m in masks[1:]:
        w = [_dot3(jnp.where(m, a, 0.0), ti) for a, ti in zip(a_list, t)]
        t = [ti - _dot3(ti, wi) for ti, wi in zip(t, w)]
    return t


def _gdn_body(qkv_ref, tail_ref, cw_ref, alog_ref, dtb_ref, s0_ref, c0_ref, o_ref, so_ref,
              xp_ref, s_ref, *, n_valid):
    c = pl.program_id(1)
    L = CHUNK

    @pl.when(c == 0)
    def _():
        xp_ref[0:SUBLANES, :] = c0_ref[...]
        s_ref[...] = s0_ref[...]

    xp_ref[SUBLANES:SUBLANES + L, :] = qkv_ref[...]
    cw = cw_ref[...]
    base = SUBLANES - (CONV_W - 1)
    y = xp_ref[base:base + L, :] * cw[0:1, :]
    for i in range(1, CONV_W):
        y = y + xp_ref[base + i:base + i + L, :] * cw[i:i + 1, :]
    xp_ref[0:SUBLANES, :] = xp_ref[L:L + SUBLANES, :]
    y = y * jax.nn.sigmoid(y)

    row = lax.broadcasted_iota(jnp.int32, (L, L), 0)
    colm = lax.broadcasted_iota(jnp.int32, (L, L), 1)
    incl = row >= colm
    strict = row > colm
    eye = jnp.where(row == colm, 1.0, 0.0).astype(f32)
    masks = _block_masks(row, colm)
    ltri = jnp.where(incl, 1.0, 0.0).astype(bf16)
    utri = jnp.where(row <= colm, 1.0, 0.0).astype(bf16)

    t = tail_ref[...]
    valid = (c * L + row) < n_valid
    beta = jnp.where(valid, jax.nn.sigmoid(t), 0.0)
    z = t + dtb_ref[...]
    softplus = jnp.maximum(z, 0.0) + jnp.log(1.0 + jnp.exp(-jnp.abs(z)))
    g = jnp.where(valid, -jnp.exp(alog_ref[...]) * softplus, 0.0)
    g_col = _dot_exact_lhs(ltri, g)
    g_row = _dot_exact_rhs(g.T, utri)
    gam = jnp.exp(g_col)
    g_last = g_col[L - 1:L, :]
    kdec = jnp.exp(g_last - g_col)
    s_scale = jnp.exp(g_last)

    heads = range(GDN_HEADS)
    lane = [GDN_HEADS + h for h in heads]
    l2 = lambda x: x * lax.rsqrt(jnp.sum(x * x, axis=-1, keepdims=True) + EPS)
    q = [l2(y[:, h * GDN_DK:(h + 1) * GDN_DK]) * (GDN_DK ** -0.5) for h in heads]
    k = [l2(y[:, GDN_QK_W + h * GDN_DK:GDN_QK_W + (h + 1) * GDN_DK]) for h in heads]
    v = [y[:, 2 * GDN_QK_W + h * GDN_DV:2 * GDN_QK_W + (h + 1) * GDN_DV] for h in heads]
    beta_c = [beta[:, h:h + 1] for h in heads]
    gam_c = [gam[:, gl:gl + 1] for gl in lane]
    dec = [jnp.exp(jnp.where(incl, g_col[:, gl:gl + 1] - g_row[gl:gl + 1, :], NEG)) for gl in lane]
    kb = [k[h] * beta_c[h] for h in heads]
    k_bf = [x.astype(bf16) for x in k]
    a = [_dot_nt(kb[h], k_bf[h]) * jnp.where(strict, dec[h], 0.0) for h in heads]
    qk = [_dot_nt(q[h], k_bf[h]) * dec[h] for h in heads]
    tinv = _tri_inverse(a, eye, masks)
    sol = [_dot3(tinv[h], jnp.concatenate([kb[h] * gam_c[h], v[h] * beta_c[h]], axis=1)) for h in heads]
    s_old = [s_ref[h] for h in heads]
    u = [sol[h][:, GDN_DK:] - _dot(sol[h][:, :GDN_DK], s_old[h]) for h in heads]
    o = [_dot(q[h] * gam_c[h], s_old[h]) + _dot(qk[h], u[h]) for h in heads]
    for h in heads:
        o_ref[:, h * GDN_DV:(h + 1) * GDN_DV] = o[h]
    for h, gl in zip(heads, lane):
        kd = (k[h] * kdec[:, gl:gl + 1]).T
        s_ref[h] = s_scale[:, gl:gl + 1] * s_old[h] + _dot(kd, u[h])

    @pl.when(c == pl.num_programs(1) - 1)
    def _():
        so_ref[...] = s_ref[...]


def _gdn(qkv_src, tail, conv_w, alog_row, dtb_row, s0, c0, layer, n_seq, n_chunks, n_valid):
    rows = n_seq * n_chunks * CHUNK
    return pl.pallas_call(
        functools.partial(_gdn_body, n_valid=n_valid),
        out_shape=(jax.ShapeDtypeStruct((rows, GDN_V_W), f32),
                   jax.ShapeDtypeStruct((n_seq, GDN_HEADS, GDN_DK, GDN_DV), f32)),
        grid=(n_seq, n_chunks),
        in_specs=[
            pl.BlockSpec((CHUNK, CONV_CH), lambda s, c: (s * n_chunks + c, 0)),
            pl.BlockSpec((CHUNK, LANES), lambda s, c: (s * n_chunks + c, 0)),
            pl.BlockSpec((None, CONV_W, CONV_CH), lambda s, c: (layer, 0, 0)),
            pl.BlockSpec((None, 1, LANES), lambda s, c: (layer, 0, 0)),
            pl.BlockSpec((None, 1, LANES), lambda s, c: (layer, 0, 0)),
            pl.BlockSpec((None, GDN_HEADS, GDN_DK, GDN_DV), lambda s, c: (s, 0, 0, 0)),
            pl.BlockSpec((None, SUBLANES, CONV_CH), lambda s, c: (s, 0, 0)),
        ],
        out_specs=(pl.BlockSpec((CHUNK, GDN_V_W), lambda s, c: (s * n_chunks + c, 0)),
                   pl.BlockSpec((None, GDN_HEADS, GDN_DK, GDN_DV), lambda s, c: (s, 0, 0, 0))),
        scratch_shapes=[pltpu.VMEM((CHUNK + SUBLANES, CONV_CH), f32),
                        pltpu.VMEM((GDN_HEADS, GDN_DK, GDN_DV), f32)],
        compiler_params=_params("parallel", "arbitrary"),
        name="gdn",
    )(qkv_src, tail, conv_w, alog_row, dtb_row, s0, c0)


def _lambda(lv_ref, lam_init):
    lv = lv_ref[...]
    a = jnp.sum(lv[0:1, :] * lv[1:2, :], axis=-1, keepdims=True)
    b = jnp.sum(lv[2:3, :] * lv[3:4, :], axis=-1, keepdims=True)
    return jnp.exp(a) - jnp.exp(b) + lam_init


ATTN_ROW_TILE = 32


def _attn_body(lv_ref, q_ref, k_ref, v_ref, o_ref, q2_ref, s_ref, p_ref, m_ref, l_ref, al_ref, ps_ref,
               acc_ref, *, tq, lam_init):
    qi = pl.program_id(2)
    rt = ATTN_ROW_TILE
    q = q_ref[...].astype(f32)
    lane = lax.broadcasted_iota(jnp.int32, (tq, LANES), 1)
    q2_ref[0] = jnp.where(lane < DIFF_DQK, q, 0.0).astype(bf16)
    q2_ref[1] = jnp.where(lane >= DIFF_DQK, q, 0.0).astype(bf16)
    m_ref[...] = jnp.full_like(m_ref, NEG)
    l_ref[...] = jnp.zeros_like(l_ref)
    acc_ref[...] = jnp.zeros_like(acc_ref)

    def scores(kj, slot):
        kb = k_ref[pl.ds(pl.multiple_of(kj * tq, tq), tq), :]
        for c in range(2):
            s_ref[slot, c] = lax.dot_general(q2_ref[c], kb, (((1,), (1,)), ((), ())),
                                             preferred_element_type=f32)

    def softmax_pv(kj, slot, masked):
        vb = v_ref[pl.ds(pl.multiple_of(kj * tq, tq), tq), :]
        tiles = [slice(r * rt, (r + 1) * rt) for r in range(tq // rt)]
        lanes = [slice(j * LANES, (j + 1) * LANES) for j in range(tq // LANES)]
        wide = lambda x: jnp.broadcast_to(x, (rt, LANES))
        for c in range(2):
            for rows in tiles:
                s = s_ref[slot, c, rows, :]
                if masked:
                    ri = rows.start + lax.broadcasted_iota(jnp.int32, (rt, tq), 0)
                    ci = lax.broadcasted_iota(jnp.int32, (rt, tq), 1)
                    s = jnp.where(ri >= ci, s, NEG)
                    s_ref[slot, c, rows, :] = s
                al_ref[c, rows, :] = wide(jnp.max(s, axis=-1, keepdims=True))
        for c in range(2):
            m_old = m_ref[c]
            m_new = jnp.maximum(m_old, al_ref[c])
            m_ref[c] = m_new
            al_ref[c] = jnp.exp(m_old - m_new)
            for rows in tiles:
                m_t = m_ref[c, rows, :]
                ps = [jnp.exp(s_ref[slot, c, rows, ln] - m_t) for ln in lanes]
                for ln, pj in zip(lanes, ps):
                    p_ref[c, rows, ln] = pj.astype(bf16)
                ps_ref[c, rows, :] = wide(jnp.sum(functools.reduce(jnp.add, ps), axis=-1, keepdims=True))
            l_ref[c] = al_ref[c] * l_ref[c] + ps_ref[c]
            acc_ref[c] = acc_ref[c] * al_ref[c] + jnp.dot(p_ref[c], vb, preferred_element_type=f32)

    scores(0, 0)

    def block(kj, slot):
        scores(kj + 1, 1 - slot)
        softmax_pv(kj, slot, False)

    def pair(j2, carry):
        block(2 * j2, 0)
        block(2 * j2 + 1, 1)
        return carry

    lax.fori_loop(0, qi // 2, pair, 0)
    odd = qi % 2 == 1

    @pl.when(odd)
    def _():
        block(qi - 1, 0)
        softmax_pv(qi, 1, True)

    @pl.when(jnp.logical_not(odd))
    def _():
        softmax_pv(qi, 0, True)

    lam = _lambda(lv_ref, lam_init)
    o_ref[...] = acc_ref[0] / l_ref[0] - lam * (acc_ref[1] / l_ref[1])


def _attn(lam_vecs, q, k, v, layer, lam_init, n_batch, rows_per_seq):
    tq = _pick_tile(rows_per_seq, 384, CHUNK)
    nq = rows_per_seq // tq
    assert tq % ATTN_ROW_TILE == 0
    return pl.pallas_call(
        functools.partial(_attn_body, tq=tq, lam_init=lam_init),
        out_shape=jax.ShapeDtypeStruct((n_batch * rows_per_seq, DIFF_HEADS * DIFF_DV), f32),
        grid=(n_batch, DIFF_HEADS, nq),
        in_specs=[
            pl.BlockSpec((None, 4, DIFF_DQK), lambda b, h, i: (layer, 0, 0)),
            pl.BlockSpec((tq, LANES), lambda b, h, i: (b * nq + i, h)),
            pl.BlockSpec((rows_per_seq, LANES), lambda b, h, i: (b, h)),
            pl.BlockSpec((rows_per_seq, LANES), lambda b, h, i: (b, h)),
        ],
        out_specs=pl.BlockSpec((tq, DIFF_DV), lambda b, h, i: (b * nq + i, h)),
        scratch_shapes=[pltpu.VMEM((2, tq, LANES), bf16),
                        pltpu.VMEM((2, 2, tq, tq), f32),
                        pltpu.VMEM((2, tq, tq), bf16),
                        pltpu.VMEM((2, tq, LANES), f32), pltpu.VMEM((2, tq, LANES), f32),
                        pltpu.VMEM((2, tq, LANES), f32), pltpu.VMEM((2, tq, LANES), f32),
                        pltpu.VMEM((2, tq, DIFF_DV), f32)],
        compiler_params=_params("parallel", "parallel", "arbitrary"),
        name="diff_attn",
    )(lam_vecs, q, k, v)


def _decode_body(pt_ref, lv_ref, q_ref, *refs, n_pages_step, lam_init, n_new):
    del pt_ref
    k_refs = refs[:n_pages_step]
    v_refs = refs[n_pages_step:2 * n_pages_step]
    kn_ref, vn_ref, o_ref, m_ref, l_ref, acc_ref = refs[2 * n_pages_step:]
    step = pl.program_id(1)
    nq = q_ref.shape[0]

    @pl.when(step == 0)
    def _():
        m_ref[...] = jnp.full_like(m_ref, NEG)
        l_ref[...] = jnp.zeros_like(l_ref)
        acc_ref[...] = jnp.zeros_like(acc_ref)

    qt = q_ref[...]

    def process(ks, vs, mask):
        ss = [jnp.dot(qt, kr[...].astype(bf16), preferred_element_type=f32) for kr in ks]
        if mask is not None:
            ss = [jnp.where(mask, s, NEG) for s in ss]
        mx = functools.reduce(jnp.maximum, [jnp.max(s, axis=1, keepdims=True) for s in ss])
        m_old = m_ref[...]
        m_new = jnp.maximum(m_old, mx)
        alpha = jnp.exp(m_old - m_new)
        ps = [jnp.exp(s - m_new) for s in ss]
        l_ref[...] = alpha * l_ref[...] + functools.reduce(
            jnp.add, [jnp.sum(p, axis=1, keepdims=True) for p in ps])
        m_ref[...] = m_new
        for h in range(DIFF_HEADS):
            rows = slice(h * SUBLANES, (h + 1) * SUBLANES)
            acc = acc_ref[h] * alpha[rows]
            for p, vr in zip(ps, vs):
                vh = vr[pl.ds(h, PAGE_SIZE, stride=DIFF_HEADS), :].astype(bf16)
                acc = acc + jnp.dot(p[rows].astype(bf16), vh, preferred_element_type=f32)
            acc_ref[h] = acc

    process(k_refs, v_refs, None)

    @pl.when(step == pl.num_programs(1) - 1)
    def _():
        tok = lax.broadcasted_iota(jnp.int32, (nq, PAGE_SIZE), 0) % n_new
        key = lax.broadcasted_iota(jnp.int32, (nq, PAGE_SIZE), 1)
        process([kn_ref], [vn_ref], key <= tok)
        lam = _lambda(lv_ref, lam_init)
        linv = 1.0 / l_ref[...]
        for h in range(DIFF_HEADS):
            a = acc_ref[h] * linv[h * SUBLANES:(h + 1) * SUBLANES]
            o_ref[:, h * DIFF_DV:(h + 1) * DIFF_DV] = a - lam * pltpu.roll(a, n_new, 0)


def _decode(page_table, lam_vecs, qt, cache_kt, cache_v, kt_new, v_new, layer, lam_init, n_new):
    n_seq, n_pages = page_table.shape
    g = PAGES_PER_STEP
    nq = qt.shape[1]
    assert n_pages % g == 0 and 2 * n_new == SUBLANES and nq == DIFF_HEADS * SUBLANES
    kr, vr = cache_kt.shape[2], cache_v.shape[2]

    def page_spec(rows, u):
        return pl.BlockSpec((None, None, rows, LANES),
                            lambda b, s, pt: (layer, pt[b * n_pages + s * g + u], 0, 0))

    new_spec = lambda rows: pl.BlockSpec((None, rows, LANES), lambda b, s, pt: (b, 0, 0))
    grid_spec = pltpu.PrefetchScalarGridSpec(
        num_scalar_prefetch=1,
        grid=(n_seq, n_pages // g),
        in_specs=[pl.BlockSpec((None, 4, DIFF_DQK), lambda b, s, pt: (layer, 0, 0)),
                  pl.BlockSpec((None, nq, kr), lambda b, s, pt: (b, 0, 0))]
                 + [page_spec(kr, u) for u in range(g)] + [page_spec(vr, u) for u in range(g)]
                 + [new_spec(kr), new_spec(vr)],
        out_specs=pl.BlockSpec((None, SUBLANES, DIFF_HEADS * DIFF_DV), lambda b, s, pt: (b, 0, 0)),
        scratch_shapes=[pltpu.VMEM((nq, 1), f32), pltpu.VMEM((nq, 1), f32),
                        pltpu.VMEM((DIFF_HEADS, SUBLANES, DIFF_DV), f32)],
    )
    return pl.pallas_call(
        functools.partial(_decode_body, n_pages_step=g, lam_init=lam_init, n_new=n_new),
        out_shape=jax.ShapeDtypeStruct((n_seq, SUBLANES, DIFF_HEADS * DIFF_DV), f32),
        grid_spec=grid_spec,
        compiler_params=_params("parallel", "arbitrary"),
        name="decode_attn",
    )(page_table.reshape(-1), lam_vecs, qt, *([cache_kt] * g), *([cache_v] * g), kt_new, v_new)


def _merge_body(x_ref, og_ref, z_ref, od_ref, ga_ref, gb_ref, gn_ref, sn_ref, wbg_ref, wbd_ref,
                wo_ref, o_ref, yg_ref, yd_ref, *, d_scale):
    gn, sn = gn_ref[...], sn_ref[...]
    for h in range(GDN_HEADS):
        sl = slice(h * GDN_DV, (h + 1) * GDN_DV)
        zz = z_ref[:, sl]
        yg_ref[:, sl] = (_rms(og_ref[:, sl], gn) * (zz * jax.nn.sigmoid(zz))).astype(bf16)
    for h in range(DIFF_HEADS):
        sl = slice(h * DIFF_DV, (h + 1) * DIFF_DV)
        yd_ref[:, sl] = (_rms(od_ref[:, sl], sn) * d_scale).astype(bf16)
    mix = (jax.nn.sigmoid(ga_ref[...]) * jnp.dot(yg_ref[...], wbg_ref[...], preferred_element_type=f32)
           + jax.nn.sigmoid(gb_ref[...]) * jnp.dot(yd_ref[...], wbd_ref[...], preferred_element_type=f32))
    o_ref[...] = x_ref[...] + jnp.dot(mix.astype(bf16), wo_ref[...], preferred_element_type=f32)


def _merge(x, o_g, p, o_d, gn, sn, wbg, wbd, wo, layer, lam_init):
    m, d = x.shape
    tm = _pick_tile(m, 256)
    row = lambda i: (i, 0)
    wspec = lambda r, c: pl.BlockSpec((None, r, c), lambda i: (layer, 0, 0), pipeline_mode=pl.Buffered(1))
    return pl.pallas_call(
        functools.partial(_merge_body, d_scale=1.0 - lam_init),
        out_shape=jax.ShapeDtypeStruct((m, d), f32),
        grid=(m // tm,),
        in_specs=[
            pl.BlockSpec((tm, d), row),
            pl.BlockSpec((tm, GDN_V_W), row),
            pl.BlockSpec((tm, GDN_V_W), lambda i: (i, COL_Z)),
            pl.BlockSpec((tm, DIFF_W), row),
            pl.BlockSpec((tm, d), lambda i: (i, COL_GA * DIFF_W // d)),
            pl.BlockSpec((tm, d), lambda i: (i, COL_GB * DIFF_W // d)),
            pl.BlockSpec((None, 1, GDN_DV), lambda i: (layer, 0, 0)),
            pl.BlockSpec((None, 1, DIFF_DV), lambda i: (layer, 0, 0)),
            wspec(GDN_V_W, d), wspec(DIFF_W, d), wspec(d, d),
        ],
        out_specs=pl.BlockSpec((tm, d), row),
        scratch_shapes=[pltpu.VMEM((tm, GDN_V_W), bf16), pltpu.VMEM((tm, DIFF_W), bf16)],
        compiler_params=_params("parallel"),
        name="merge",
    )(x, o_g, p, o_d, p, p, gn, sn, wbg, wbd, wo)


def _final_body(x_ref, w_ref, o_ref):
    o_ref[...] = _rms(x_ref[...], w_ref[...])


def _final_norm(x, w):
    m, d = x.shape
    tm = _pick_tile(m, 768)
    return pl.pallas_call(
        _final_body,
        out_shape=jax.ShapeDtypeStruct((m, d), f32),
        grid=(m // tm,),
        in_specs=[pl.BlockSpec((tm, d), lambda i: (i, 0)), pl.BlockSpec((1, d), lambda i: (0, 0))],
        out_specs=pl.BlockSpec((tm, d), lambda i: (i, 0)),
        compiler_params=_params("parallel"),
        name="final_norm",
    )(x, w)


def kernel(x_prompt, x_sample, cache_k, cache_v, state_delta, state_conv, page_table, meta_tokens,
           ffn1_norm, ffn1_gate, ffn1_up, ffn1_down, mix_norm, w_in, conv_w, gdn_a_log, gdn_dt_bias,
           gdn_out_norm, diff_lambda, diff_subln, w_branch_gdn, w_branch_diff, w_out, ffn2_norm,
           ffn2_gate, ffn2_up, ffn2_down, final_norm):
    nb, seq, d = x_prompt.shape
    ns, ls, _ = x_sample.shape
    depth = w_in.shape[0]
    n_tok = seq + N_META
    rps = -(-n_tok // CHUNK) * CHUNK
    n_smp = ns * ls
    assert rps - n_tok >= n_smp and d == 2 * DIFF_W
    past = page_table.shape[1] * PAGE_SIZE
    n_pool = cache_k.shape[1]

    x = jnp.concatenate([jnp.broadcast_to(meta_tokens[None], (nb, N_META, d)), x_prompt,
                         jnp.zeros((nb, rps - n_tok, d), f32)], axis=1).reshape(nb * rps, d)
    s_lo, s_hi = n_tok, n_tok + n_smp
    x = x.at[s_lo:s_hi].set(x_sample.reshape(n_smp, d))

    pos = jnp.tile(jnp.arange(rps, dtype=jnp.int32), nb)
    pos = pos.at[s_lo:s_hi].set(jnp.tile(past + jnp.arange(ls, dtype=jnp.int32), ns))
    tabs = _rope_tables(pos)

    gate_lo = CONV_CH + GDN_V_W
    gate_hi = gate_lo + 2 * GDN_HEADS
    diff_hi = gate_hi + 3 * DIFF_W
    w_main = jnp.concatenate([w_in[:, :, :gate_lo], w_in[:, :, diff_hi:], w_in[:, :, gate_hi:diff_hi]],
                             axis=-1).astype(bf16)
    w_tail = jnp.pad(w_in[:, :, gate_lo:gate_hi], ((0, 0), (0, 0), (0, LANES - 2 * GDN_HEADS))).astype(bf16)
    cast = lambda w: w.astype(bf16)
    f1g, f1u, f1d, f2g, f2u, f2d = map(cast, (ffn1_gate, ffn1_up, ffn1_down, ffn2_gate, ffn2_up, ffn2_down))
    wbg, wbd, wo = map(cast, (w_branch_gdn, w_branch_diff, w_out))
    row3 = lambda w: w.reshape(depth, 1, -1)
    n1, n2, nm, gn, sn = map(row3, (ffn1_norm, ffn2_norm, mix_norm, gdn_out_norm, diff_subln))
    lane_pad = lambda w: jnp.pad(w, ((0, 0), (GDN_HEADS, LANES - 2 * GDN_HEADS))).reshape(depth, 1, LANES)
    alog_row, dtb_row = lane_pad(gdn_a_log), lane_pad(gdn_dt_bias)

    ck = jnp.transpose(cache_k, (0, 1, 3, 4, 5, 2)).reshape(depth, n_pool, DIFF_W, PAGE_SIZE)
    cv = cache_v.reshape(depth, n_pool, PAGE_SIZE * DIFF_HEADS, DIFF_DV)
    zero_state = jnp.zeros((nb, GDN_HEADS, GDN_DK, GDN_DV), f32)
    zero_conv = jnp.zeros((nb, SUBLANES, CONV_CH), f32)
    conv_pad = jnp.pad(state_conv, ((0, 0), (0, 0), (SUBLANES - (CONV_W - 1), 0), (0, 0)))

    def smp_chunks(a):
        w = a.shape[-1]
        return jnp.pad(a.reshape(ns, ls, w), ((0, 0), (0, CHUNK - ls), (0, 0))).reshape(ns * CHUNK, w)

    outs = [[] for _ in range(8)]
    for l in range(depth):
        lam_init = 0.8 - 0.6 * math.exp(-0.3 * l)
        x = _ffn(x, n1, f1g, f1u, f1d, l)
        p, tail = _inproj(x, nm, w_main, w_tail, l)
        q_bf, k_bf, v_bf, k_f = _rope(p, tabs)

        o_g, st_p = _gdn(p, tail, conv_w, alog_row, dtb_row, zero_state, zero_conv, l, nb, rps // CHUNK, n_tok)
        qkv_s = p[s_lo:s_hi, :CONV_CH]
        og_s, st_s = _gdn(smp_chunks(qkv_s), smp_chunks(tail[s_lo:s_hi]), conv_w, alog_row, dtb_row,
                          state_delta[l], conv_pad[l], l, ns, 1, ls)
        o_g = o_g.at[s_lo:s_hi].set(og_s.reshape(ns, CHUNK, GDN_V_W)[:, :ls].reshape(n_smp, GDN_V_W))

        o_d = _attn(diff_lambda, q_bf, k_bf, v_bf, l, lam_init, nb, rps)
        q_s = q_bf[s_lo:s_hi].reshape(ns, ls, DIFF_HEADS, 2, DIFF_DQK)
        hc = DIFF_HEADS * 2
        eye_hc = jnp.eye(hc, dtype=bf16).reshape(DIFF_HEADS, 2, 1, DIFF_HEADS, 2, 1)
        qt = (jnp.transpose(q_s, (0, 2, 3, 1, 4))[:, :, :, :, None, None, :]
              * eye_hc[None, :, :, :, :, :, :]).reshape(ns, hc * ls, DIFF_W)
        kt_new = jnp.pad(jnp.transpose(k_f[s_lo:s_hi].reshape(ns, ls, DIFF_W), (0, 2, 1)),
                         ((0, 0), (0, 0), (0, PAGE_SIZE - ls)))
        v_s = p[s_lo:s_hi, COL_DV * DIFF_W:(COL_DV + 1) * DIFF_W]
        v_new = jnp.pad(v_s.reshape(ns, ls, DIFF_HEADS, DIFF_DV),
                        ((0, 0), (0, PAGE_SIZE - ls), (0, 0), (0, 0))).reshape(ns, PAGE_SIZE * DIFF_HEADS, DIFF_DV)
        od_s = _decode(page_table, diff_lambda, qt, ck, cv, kt_new, v_new, l, lam_init, ls)
        o_d = o_d.at[s_lo:s_hi].set(od_s[:, :ls].reshape(n_smp, DIFF_W))

        x = _merge(x, o_g, p, o_d, gn, sn, wbg, wbd, wo, l, lam_init)
        x = _ffn(x, n2, f2g, f2u, f2d, l)

        seq_view = lambda a: a.reshape(nb, rps, a.shape[-1])
        outs[0].append(seq_view(k_f)[:, :n_tok].reshape(nb, n_tok, DIFF_HEADS, 2, DIFF_DQK))
        outs[1].append(seq_view(p[:, COL_DV * DIFF_W:(COL_DV + 1) * DIFF_W])[:, :n_tok]
                       .reshape(nb, n_tok, DIFF_HEADS, DIFF_DV))
        outs[2].append(st_p)
        outs[3].append(seq_view(p)[:, n_tok - (CONV_W - 1):n_tok, :CONV_CH])
        outs[4].append(k_f[s_lo:s_hi].reshape(ns, ls, DIFF_HEADS, 2, DIFF_DQK))
        outs[5].append(v_s.reshape(ns, ls, DIFF_HEADS, DIFF_DV))
        outs[6].append(st_s)
        outs[7].append(jnp.concatenate([state_conv[l], qkv_s.reshape(ns, ls, CONV_CH)], axis=1)[:, -(CONV_W - 1):])

    y = _final_norm(x, final_norm.reshape(1, d))
    y_prompt = y.reshape(nb, rps, d)[:, N_META:n_tok]
    y_sample = y[s_lo:s_hi].reshape(ns, ls, d)
    return (y_prompt, y_sample) + tuple(jnp.stack(o) for o in outs)
```

```python
import functools
import math

import jax
import jax.numpy as jnp
from jax import lax
from jax.experimental import pallas as pl
from jax.experimental.pallas import tpu as pltpu

N_META = 16
GDN_HEADS = 8
GDN_DK = 128
GDN_DV = 128
CONV_W = 4
DIFF_HEADS = 8
DIFF_DQK = 64
DIFF_DV = 128
ROPE_THETA = 500000.0
ROPE_DIM = DIFF_DQK // 4
PAGE_SIZE = 128
EPS = 1e-6

CHUNK = 128
LANES = 128
SUBLANES = 8
NEG = -1e30
VMEM_LIMIT = 56 * 1024 * 1024
PAGES_PER_STEP = 8

GDN_QK_W = GDN_HEADS * GDN_DK
GDN_V_W = GDN_HEADS * GDN_DV
CONV_CH = 2 * GDN_QK_W + GDN_V_W
DIFF_W = DIFF_HEADS * 2 * DIFF_DQK
COL_Z, COL_GA, COL_GB, COL_DQ, COL_DK, COL_DV = 3, 4, 6, 8, 9, 10

f32 = jnp.float32
bf16 = jnp.bfloat16


def _params(*sem):
    return pltpu.CompilerParams(dimension_semantics=sem, vmem_limit_bytes=VMEM_LIMIT)


def _pick_tile(n, target, mult=SUBLANES):
    best = None
    for t in range(mult, min(n, target) + 1, mult):
        if n % t == 0:
            best = t
    assert best is not None, (n, target, mult)
    return best


def _rms(x, w):
    return x * lax.rsqrt(jnp.mean(x * x, axis=-1, keepdims=True) + EPS) * w


def _dot(a, b):
    return jnp.dot(a.astype(bf16), b.astype(bf16), preferred_element_type=f32)


def _dot_nt(a, b):
    return lax.dot_general(a.astype(bf16), b.astype(bf16), (((1,), (1,)), ((), ())),
                           preferred_element_type=f32)


def _dot_exact_lhs(a_bf, x):
    x1 = x.astype(bf16)
    r1 = x - x1.astype(f32)
    x2 = r1.astype(bf16)
    x3 = (r1 - x2.astype(f32)).astype(bf16)
    d = functools.partial(jnp.dot, preferred_element_type=f32)
    return d(a_bf, x1) + (d(a_bf, x2) + d(a_bf, x3))


def _dot_exact_rhs(x, b_bf):
    x1 = x.astype(bf16)
    r1 = x - x1.astype(f32)
    x2 = r1.astype(bf16)
    x3 = (r1 - x2.astype(f32)).astype(bf16)
    d = functools.partial(jnp.dot, preferred_element_type=f32)
    return d(x1, b_bf) + (d(x2, b_bf) + d(x3, b_bf))


def _ffn_body(x_ref, nw_ref, wg_ref, wu_ref, wd_ref, o_ref, h_ref):
    j = pl.program_id(1)

    @pl.when(j == 0)
    def _():
        h_ref[...] = _rms(x_ref[...], nw_ref[...]).astype(bf16)
        o_ref[...] = jnp.zeros_like(o_ref)

    h = h_ref[...]
    g = jnp.dot(h, wg_ref[...], preferred_element_type=f32)
    u = jnp.dot(h, wu_ref[...], preferred_element_type=f32)
    a = (g * jax.nn.sigmoid(g) * u).astype(bf16)
    o_ref[...] += jnp.dot(a, wd_ref[...], preferred_element_type=f32)

    @pl.when(j == pl.num_programs(1) - 1)
    def _():
        o_ref[...] = x_ref[...] + 0.5 * o_ref[...]


def _ffn(x, nw, wg, wu, wd, layer):
    m, d = x.shape
    ff = wg.shape[-1]
    tm = _pick_tile(m, 768)
    tf = _pick_tile(ff, 512, LANES)
    return pl.pallas_call(
        _ffn_body,
        out_shape=jax.ShapeDtypeStruct((m, d), f32),
        grid=(m // tm, ff // tf),
        in_specs=[
            pl.BlockSpec((tm, d), lambda i, j: (i, 0)),
            pl.BlockSpec((None, 1, d), lambda i, j: (layer, 0, 0)),
            pl.BlockSpec((None, d, tf), lambda i, j: (layer, 0, j)),
            pl.BlockSpec((None, d, tf), lambda i, j: (layer, 0, j)),
            pl.BlockSpec((None, tf, d), lambda i, j: (layer, j, 0)),
        ],
        out_specs=pl.BlockSpec((tm, d), lambda i, j: (i, 0)),
        scratch_shapes=[pltpu.VMEM((tm, d), bf16)],
        compiler_params=_params("parallel", "arbitrary"),
        name="ffn",
    )(x, nw, wg, wu, wd)


def _inproj_body(x_ref, nw_ref, w_ref, wt_ref, o_ref, t_ref, h_ref):
    @pl.when(pl.program_id(1) == 0)
    def _():
        h = _rms(x_ref[...], nw_ref[...]).astype(bf16)
        h_ref[...] = h
        t_ref[...] = jnp.dot(h, wt_ref[...], preferred_element_type=f32)

    o_ref[...] = jnp.dot(h_ref[...], w_ref[...], preferred_element_type=f32)


def _inproj(x, nw, w_main, w_tail, layer):
    m, d = x.shape
    n = w_main.shape[-1]
    tm = _pick_tile(m, 768)
    tn = DIFF_W
    return pl.pallas_call(
        _inproj_body,
        out_shape=(jax.ShapeDtypeStruct((m, n), f32), jax.ShapeDtypeStruct((m, LANES), f32)),
        grid=(m // tm, n // tn),
        in_specs=[
            pl.BlockSpec((tm, d), lambda i, j: (i, 0)),
            pl.BlockSpec((None, 1, d), lambda i, j: (layer, 0, 0)),
            pl.BlockSpec((None, d, tn), lambda i, j: (layer, 0, j)),
            pl.BlockSpec((None, d, LANES), lambda i, j: (layer, 0, 0)),
        ],
        out_specs=(pl.BlockSpec((tm, tn), lambda i, j: (i, j)),
                   pl.BlockSpec((tm, LANES), lambda i, j: (i, 0))),
        scratch_shapes=[pltpu.VMEM((tm, d), bf16)],
        compiler_params=_params("parallel", "arbitrary"),
        name="inproj",
    )(x, nw, w_main, w_tail)


def _rope_body(dq_ref, dk_ref, dv_ref, c_ref, s1_ref, s2_ref, q_ref, kb_ref, vb_ref, kf_ref):
    c, s1, s2 = c_ref[...], s1_ref[...], s2_ref[...]
    for h in range(DIFF_HEADS):
        sl = slice(h * LANES, (h + 1) * LANES)
        for src, scale, dsts in ((dq_ref, DIFF_DQK ** -0.5, (q_ref,)), (dk_ref, None, (kb_ref, kf_ref))):
            x = src[:, sl]
            half = ROPE_DIM // 2
            r = x * c + pltpu.roll(x, half, 1) * s1 + pltpu.roll(x, LANES - half, 1) * s2
            if scale is not None:
                r = r * scale
            for dst in dsts:
                dst[:, sl] = r.astype(dst.dtype)
    vb_ref[...] = dv_ref[...].astype(bf16)


def _rope(p, tabs):
    m = p.shape[0]
    tr = _pick_tile(m, 384, 16)
    row = lambda i: (i, 0)
    col = lambda cb: pl.BlockSpec((tr, DIFF_W), lambda i: (i, cb))
    tab = pl.BlockSpec((tr, LANES), row)
    return pl.pallas_call(
        _rope_body,
        out_shape=(jax.ShapeDtypeStruct((m, DIFF_W), bf16),) * 3 + (jax.ShapeDtypeStruct((m, DIFF_W), f32),),
        grid=(m // tr,),
        in_specs=[col(COL_DQ), col(COL_DK), col(COL_DV), tab, tab, tab],
        out_specs=(pl.BlockSpec((tr, DIFF_W), row),) * 4,
        compiler_params=_params("parallel"),
        name="rope",
    )(p, p, p, *tabs)


def _rope_tables(pos):
    half = ROPE_DIM // 2
    inv = ROPE_THETA ** (-jnp.arange(half, dtype=f32) / half)
    ang = pos.astype(f32)[:, None] * inv[None, :]
    cos, sin = jnp.cos(ang), jnp.sin(ang)
    n = pos.shape[0]
    rest = DIFF_DQK - ROPE_DIM
    one, zero, zh = jnp.ones((n, rest), f32), jnp.zeros((n, rest), f32), jnp.zeros((n, half), f32)
    c = jnp.concatenate([cos, cos, one], axis=1)
    s1 = jnp.concatenate([zh, sin, zero], axis=1)
    s2 = jnp.concatenate([-sin, zh, zero], axis=1)
    rep = LANES // DIFF_DQK
    return tuple(jnp.tile(t, (1, rep)) for t in (c, s1, s2))


INV_BASE = 16


def _block_masks(row, colm):
    masks = [(row // INV_BASE) == (colm // INV_BASE)]
    nb = INV_BASE
    while nb < row.shape[0]:
        masks.append(((row // nb) == (colm // nb) + 1) & ((colm // nb) % 2 == 0))
        nb *= 2
    return masks


def _tri_inverse(a_list, eye, masks):
    n = a_list[0].shape[0]
    ad = [jnp.where(masks[0], a, 0.0) for a in a_list]
    t = [eye - x for x in ad]
    p = [_dot(x, x) for x in ad]
    steps = int(math.log2(INV_BASE)) - 1
    for s in range(steps):
        if s + 1 < steps:
            tp = [_dot(jnp.concatenate([ti, pi], axis=0), pi) for ti, pi in zip(t, p)]
            t = [ti + x[:n] for ti, x in zip(t, tp)]
            p = [x[n:] for x in tp]
        else:
            t = [ti + _dot(ti, pi) for ti, pi in zip(t, p)]
    for m in masks[1:]:
        w = [_dot(jnp.where(m, a, 0.0), ti) for a, ti in zip(a_list, t)]
        t = [ti - _dot(ti, wi) for ti, wi in zip(t, w)]
    return t


def _gdn_body(qkv_ref, tail_ref, cw_ref, alog_ref, dtb_ref, s0_ref, c0_ref, o_ref, so_ref,
              xp_ref, s_ref, *, n_valid):
    c = pl.program_id(1)
    L = CHUNK

    @pl.when(c == 0)
    def _():
        xp_ref[...] = c0_ref[...]
        s_ref[...] = s0_ref[...]

    u = qkv_ref[...]
    prev = xp_ref[...]
    cw = cw_ref[...]
    row8 = lax.broadcasted_iota(jnp.int32, (SUBLANES, CONV_CH), 0)
    y = u * cw[CONV_W - 1:CONV_W, :]
    for sh in range(1, CONV_W):
        us = pltpu.roll(u, sh, 0)
        top = jnp.where(row8 < sh, pltpu.roll(prev, sh, 0), us[0:SUBLANES, :])
        y = y + jnp.concatenate([top, us[SUBLANES:, :]], axis=0) * cw[CONV_W - 1 - sh:CONV_W - sh, :]
    xp_ref[...] = u[L - SUBLANES:, :]
    y = y * jax.nn.sigmoid(y)

    row = lax.broadcasted_iota(jnp.int32, (L, L), 0)
    colm = lax.broadcasted_iota(jnp.int32, (L, L), 1)
    incl = row >= colm
    strict = row > colm
    eye = jnp.where(row == colm, 1.0, 0.0).astype(f32)
    masks = _block_masks(row, colm)
    ltri = jnp.where(incl, 1.0, 0.0).astype(bf16)
    utri = jnp.where(row <= colm, 1.0, 0.0).astype(bf16)

    t = tail_ref[...]
    valid = (c * L + row) < n_valid
    beta = jnp.where(valid, jax.nn.sigmoid(t), 0.0)
    z = t + dtb_ref[...]
    softplus = jnp.maximum(z, 0.0) + jnp.log(1.0 + jnp.exp(-jnp.abs(z)))
    g = jnp.where(valid, -jnp.exp(alog_ref[...]) * softplus, 0.0)
    g_col = _dot_exact_lhs(ltri, g)
    g_row = _dot_exact_rhs(g.T, utri)
    gam = jnp.exp(g_col)
    g_last = g_col[L - 1:L, :]
    kdec = jnp.exp(g_last - g_col)
    s_scale = jnp.exp(g_last)

    heads = range(GDN_HEADS)
    lane = [GDN_HEADS + h for h in heads]
    l2 = lambda x: x * lax.rsqrt(jnp.sum(x * x, axis=-1, keepdims=True) + EPS)
    q = [l2(y[:, h * GDN_DK:(h + 1) * GDN_DK]) * (GDN_DK ** -0.5) for h in heads]
    k = [l2(y[:, GDN_QK_W + h * GDN_DK:GDN_QK_W + (h + 1) * GDN_DK]) for h in heads]
    v = [y[:, 2 * GDN_QK_W + h * GDN_DV:2 * GDN_QK_W + (h + 1) * GDN_DV] for h in heads]
    beta_c = [beta[:, h:h + 1] for h in heads]
    gam_c = [gam[:, gl:gl + 1] for gl in lane]
    dec = [jnp.exp(jnp.where(incl, g_col[:, gl:gl + 1] - g_row[gl:gl + 1, :], NEG)) for gl in lane]
    kb = [k[h] * beta_c[h] for h in heads]
    k_bf = [x.astype(bf16) for x in k]
    a = [_dot_nt(kb[h], k_bf[h]) * jnp.where(strict, dec[h], 0.0) for h in heads]
    qk = [_dot_nt(q[h], k_bf[h]) * dec[h] for h in heads]
    tinv = _tri_inverse(a, eye, masks)
    sol = [_dot(tinv[h], jnp.concatenate([kb[h] * gam_c[h], v[h] * beta_c[h]], axis=1)) for h in heads]
    s_old = [s_ref[h] for h in heads]
    u = [sol[h][:, GDN_DK:] - _dot(sol[h][:, :GDN_DK], s_old[h]) for h in heads]
    o = [_dot(q[h] * gam_c[h], s_old[h]) + _dot(qk[h], u[h]) for h in heads]
    for h in heads:
        o_ref[:, h * GDN_DV:(h + 1) * GDN_DV] = o[h]
    for h, gl in zip(heads, lane):
        kd = (k[h] * kdec[:, gl:gl + 1]).T
        s_ref[h] = s_scale[:, gl:gl + 1] * s_old[h] + _dot(kd, u[h])

    @pl.when(c == pl.num_programs(1) - 1)
    def _():
        so_ref[...] = s_ref[...]


def _gdn(qkv_src, tail, conv_w, alog_row, dtb_row, s0, c0, layer, n_seq, n_chunks, n_valid):
    rows = n_seq * n_chunks * CHUNK
    return pl.pallas_call(
        functools.partial(_gdn_body, n_valid=n_valid),
        out_shape=(jax.ShapeDtypeStruct((rows, GDN_V_W), f32),
                   jax.ShapeDtypeStruct((n_seq, GDN_HEADS, GDN_DK, GDN_DV), f32)),
        grid=(n_seq, n_chunks),
        in_specs=[
            pl.BlockSpec((CHUNK, CONV_CH), lambda s, c: (s * n_chunks + c, 0)),
            pl.BlockSpec((CHUNK, LANES), lambda s, c: (s * n_chunks + c, 0)),
            pl.BlockSpec((None, CONV_W, CONV_CH), lambda s, c: (layer, 0, 0)),
            pl.BlockSpec((None, 1, LANES), lambda s, c: (layer, 0, 0)),
            pl.BlockSpec((None, 1, LANES), lambda s, c: (layer, 0, 0)),
            pl.BlockSpec((None, GDN_HEADS, GDN_DK, GDN_DV), lambda s, c: (s, 0, 0, 0)),
            pl.BlockSpec((None, SUBLANES, CONV_CH), lambda s, c: (s, 0, 0)),
        ],
        out_specs=(pl.BlockSpec((CHUNK, GDN_V_W), lambda s, c: (s * n_chunks + c, 0)),
                   pl.BlockSpec((None, GDN_HEADS, GDN_DK, GDN_DV), lambda s, c: (s, 0, 0, 0))),
        scratch_shapes=[pltpu.VMEM((SUBLANES, CONV_CH), f32),
                        pltpu.VMEM((GDN_HEADS, GDN_DK, GDN_DV), f32)],
        compiler_params=_params("parallel", "arbitrary"),
        name="gdn",
    )(qkv_src, tail, conv_w, alog_row, dtb_row, s0, c0)


def _lambda(lv_ref, lam_init):
    lv = lv_ref[...]
    a = jnp.sum(lv[0:1, :] * lv[1:2, :], axis=-1, keepdims=True)
    b = jnp.sum(lv[2:3, :] * lv[3:4, :], axis=-1, keepdims=True)
    return jnp.exp(a) - jnp.exp(b) + lam_init


ATTN_ROW_TILE = 64


def _attn_body(lv_ref, q_ref, k_ref, v_ref, o_ref, q2_ref, s_ref, p_ref, m_ref, l_ref, al_ref, acc_ref,
               *, tq, lam_init):
    qi = pl.program_id(2)
    rt = ATTN_ROW_TILE
    q = q_ref[...].astype(f32)
    lane = lax.broadcasted_iota(jnp.int32, (tq, LANES), 1)
    q2_ref[0] = jnp.where(lane < DIFF_DQK, q, 0.0).astype(bf16)
    q2_ref[1] = jnp.where(lane >= DIFF_DQK, q, 0.0).astype(bf16)
    m_ref[...] = jnp.full_like(m_ref, NEG)
    l_ref[...] = jnp.zeros_like(l_ref)
    acc_ref[...] = jnp.zeros_like(acc_ref)

    def scores(kj, slot):
        kb = k_ref[pl.ds(pl.multiple_of(kj * tq, tq), tq), :]
        for c in range(2):
            s_ref[slot, c] = lax.dot_general(q2_ref[c], kb, (((1,), (1,)), ((), ())),
                                             preferred_element_type=f32)

    def softmax_pv(kj, slot, masked):
        vb = v_ref[pl.ds(pl.multiple_of(kj * tq, tq), tq), :]
        tiles = [slice(r * rt, (r + 1) * rt) for r in range(tq // rt)]
        lanes = [slice(j * LANES, (j + 1) * LANES) for j in range(tq // LANES)]
        wide = lambda x: jnp.broadcast_to(x, (rt, LANES))
        for c in range(2):
            for rows in tiles:
                s = s_ref[slot, c, rows, :]
                if masked:
                    ri = rows.start + lax.broadcasted_iota(jnp.int32, (rt, tq), 0)
                    ci = lax.broadcasted_iota(jnp.int32, (rt, tq), 1)
                    s = jnp.where(ri >= ci, s, NEG)
                    s_ref[slot, c, rows, :] = s
                m_old = m_ref[c, rows, :]
                m_new = jnp.maximum(m_old, wide(jnp.max(s, axis=-1, keepdims=True)))
                m_ref[c, rows, :] = m_new
                al_ref[c, rows, :] = jnp.exp(m_old - m_new)
        for c in range(2):
            for rows in tiles:
                m_t = m_ref[c, rows, :]
                ps = [jnp.exp(s_ref[slot, c, rows, ln] - m_t) for ln in lanes]
                for ln, pj in zip(lanes, ps):
                    p_ref[c, rows, ln] = pj.astype(bf16)
                psum = wide(jnp.sum(functools.reduce(jnp.add, ps), axis=-1, keepdims=True))
                l_ref[c, rows, :] = al_ref[c, rows, :] * l_ref[c, rows, :] + psum
            acc_ref[c] = acc_ref[c] * al_ref[c] + jnp.dot(p_ref[c], vb, preferred_element_type=f32)

    scores(0, 0)

    def block(kj, slot):
        scores(kj + 1, 1 - slot)
        softmax_pv(kj, slot, False)

    def pair(j2, carry):
        block(2 * j2, 0)
        block(2 * j2 + 1, 1)
        return carry

    lax.fori_loop(0, qi // 2, pair, 0)
    odd = qi % 2 == 1

    @pl.when(odd)
    def _():
        block(qi - 1, 0)
        softmax_pv(qi, 1, True)

    @pl.when(jnp.logical_not(odd))
    def _():
        softmax_pv(qi, 0, True)

    lam = _lambda(lv_ref, lam_init)
    o_ref[...] = acc_ref[0] / l_ref[0] - lam * (acc_ref[1] / l_ref[1])


def _attn(lam_vecs, q, k, v, layer, lam_init, n_batch, rows_per_seq):
    tq = _pick_tile(rows_per_seq, 384, CHUNK)
    nq = rows_per_seq // tq
    assert tq % ATTN_ROW_TILE == 0
    return pl.pallas_call(
        functools.partial(_attn_body, tq=tq, lam_init=lam_init),
        out_shape=jax.ShapeDtypeStruct((n_batch * rows_per_seq, DIFF_HEADS * DIFF_DV), f32),
        grid=(n_batch, DIFF_HEADS, nq),
        in_specs=[
            pl.BlockSpec((None, 4, DIFF_DQK), lambda b, h, i: (layer, 0, 0)),
            pl.BlockSpec((tq, LANES), lambda b, h, i: (b * nq + i, h)),
            pl.BlockSpec((rows_per_seq, LANES), lambda b, h, i: (b, h)),
            pl.BlockSpec((rows_per_seq, LANES), lambda b, h, i: (b, h)),
        ],
        out_specs=pl.BlockSpec((tq, DIFF_DV), lambda b, h, i: (b * nq + i, h)),
        scratch_shapes=[pltpu.VMEM((2, tq, LANES), bf16),
                        pltpu.VMEM((2, 2, tq, tq), f32),
                        pltpu.VMEM((2, tq, tq), bf16),
                        pltpu.VMEM((2, tq, LANES), f32), pltpu.VMEM((2, tq, LANES), f32),
                        pltpu.VMEM((2, tq, LANES), f32),
                        pltpu.VMEM((2, tq, DIFF_DV), f32)],
        compiler_params=_params("parallel", "parallel", "arbitrary"),
        name="diff_attn",
    )(lam_vecs, q, k, v)


def _decode_body(pt_ref, lv_ref, q_ref, *refs, n_pages_step, lam_init, n_new):
    del pt_ref
    k_refs = refs[:n_pages_step]
    v_refs = refs[n_pages_step:2 * n_pages_step]
    kn_ref, vn_ref, o_ref, m_ref, l_ref, acc_ref = refs[2 * n_pages_step:]
    step = pl.program_id(1)
    nq = q_ref.shape[0]

    @pl.when(step == 0)
    def _():
        m_ref[...] = jnp.full_like(m_ref, NEG)
        l_ref[...] = jnp.zeros_like(l_ref)
        acc_ref[...] = jnp.zeros_like(acc_ref)

    qt = q_ref[...]

    def process(ks, vs, mask):
        ss = [jnp.dot(qt, kr[...].astype(bf16), preferred_element_type=f32) for kr in ks]
        if mask is not None:
            ss = [jnp.where(mask, s, NEG) for s in ss]
        mx = functools.reduce(jnp.maximum, [jnp.max(s, axis=1, keepdims=True) for s in ss])
        m_old = m_ref[...]
        m_new = jnp.maximum(m_old, mx)
        alpha = jnp.exp(m_old - m_new)
        ps = [jnp.exp(s - m_new) for s in ss]
        l_ref[...] = alpha * l_ref[...] + functools.reduce(
            jnp.add, [jnp.sum(p, axis=1, keepdims=True) for p in ps])
        m_ref[...] = m_new
        for h in range(DIFF_HEADS):
            rows = slice(h * SUBLANES, (h + 1) * SUBLANES)
            acc = acc_ref[h] * alpha[rows]
            for p, vr in zip(ps, vs):
                vh = vr[pl.ds(h, PAGE_SIZE, stride=DIFF_HEADS), :].astype(bf16)
                acc = acc + jnp.dot(p[rows].astype(bf16), vh, preferred_element_type=f32)
            acc_ref[h] = acc

    process(k_refs, v_refs, None)

    @pl.when(step == pl.num_programs(1) - 1)
    def _():
        tok = lax.broadcasted_iota(jnp.int32, (nq, PAGE_SIZE), 0) % n_new
        key = lax.broadcasted_iota(jnp.int32, (nq, PAGE_SIZE), 1)
        process([kn_ref], [vn_ref], key <= tok)
        lam = _lambda(lv_ref, lam_init)
        linv = 1.0 / l_ref[...]
        for h in range(DIFF_HEADS):
            a = acc_ref[h] * linv[h * SUBLANES:(h + 1) * SUBLANES]
            o_ref[:, h * DIFF_DV:(h + 1) * DIFF_DV] = a - lam * pltpu.roll(a, n_new, 0)


def _decode(page_table, lam_vecs, qt, cache_kt, cache_v, kt_new, v_new, layer, lam_init, n_new):
    n_seq, n_pages = page_table.shape
    g = PAGES_PER_STEP
    nq = qt.shape[1]
    assert n_pages % g == 0 and 2 * n_new == SUBLANES and nq == DIFF_HEADS * SUBLANES
    kr, vr = cache_kt.shape[2], cache_v.shape[2]

    def page_spec(rows, u):
        return pl.BlockSpec((None, None, rows, LANES),
                            lambda b, s, pt: (layer, pt[b * n_pages + s * g + u], 0, 0))

    new_spec = lambda rows: pl.BlockSpec((None, rows, LANES), lambda b, s, pt: (b, 0, 0))
    grid_spec = pltpu.PrefetchScalarGridSpec(
        num_scalar_prefetch=1,
        grid=(n_seq, n_pages // g),
        in_specs=[pl.BlockSpec((None, 4, DIFF_DQK), lambda b, s, pt: (layer, 0, 0)),
                  pl.BlockSpec((None, nq, kr), lambda b, s, pt: (b, 0, 0))]
                 + [page_spec(kr, u) for u in range(g)] + [page_spec(vr, u) for u in range(g)]
                 + [new_spec(kr), new_spec(vr)],
        out_specs=pl.BlockSpec((None, SUBLANES, DIFF_HEADS * DIFF_DV), lambda b, s, pt: (b, 0, 0)),
        scratch_shapes=[pltpu.VMEM((nq, 1), f32), pltpu.VMEM((nq, 1), f32),
                        pltpu.VMEM((DIFF_HEADS, SUBLANES, DIFF_DV), f32)],
    )
    return pl.pallas_call(
        functools.partial(_decode_body, n_pages_step=g, lam_init=lam_init, n_new=n_new),
        out_shape=jax.ShapeDtypeStruct((n_seq, SUBLANES, DIFF_HEADS * DIFF_DV), f32),
        grid_spec=grid_spec,
        compiler_params=_params("parallel", "arbitrary"),
        name="decode_attn",
    )(page_table.reshape(-1), lam_vecs, qt, *([cache_kt] * g), *([cache_v] * g), kt_new, v_new)


def _merge_body(x_ref, og_ref, z_ref, od_ref, ga_ref, gb_ref, gn_ref, sn_ref, wbg_ref, wbd_ref,
                wo_ref, o_ref, yg_ref, yd_ref, *, d_scale):
    gn, sn = gn_ref[...], sn_ref[...]
    for h in range(GDN_HEADS):
        sl = slice(h * GDN_DV, (h + 1) * GDN_DV)
        zz = z_ref[:, sl]
        yg_ref[:, sl] = (_rms(og_ref[:, sl], gn) * (zz * jax.nn.sigmoid(zz))).astype(bf16)
    for h in range(DIFF_HEADS):
        sl = slice(h * DIFF_DV, (h + 1) * DIFF_DV)
        yd_ref[:, sl] = (_rms(od_ref[:, sl], sn) * d_scale).astype(bf16)
    mix = (jax.nn.sigmoid(ga_ref[...]) * jnp.dot(yg_ref[...], wbg_ref[...], preferred_element_type=f32)
           + jax.nn.sigmoid(gb_ref[...]) * jnp.dot(yd_ref[...], wbd_ref[...], preferred_element_type=f32))
    o_ref[...] = x_ref[...] + jnp.dot(mix.astype(bf16), wo_ref[...], preferred_element_type=f32)


def _merge(x, o_g, p, o_d, gn, sn, wbg, wbd, wo, layer, lam_init):
    m, d = x.shape
    tm = _pick_tile(m, 256)
    row = lambda i: (i, 0)
    wspec = lambda r, c: pl.BlockSpec((None, r, c), lambda i: (layer, 0, 0), pipeline_mode=pl.Buffered(1))
    return pl.pallas_call(
        functools.partial(_merge_body, d_scale=1.0 - lam_init),
        out_shape=jax.ShapeDtypeStruct((m, d), f32),
        grid=(m // tm,),
        in_specs=[
            pl.BlockSpec((tm, d), row),
            pl.BlockSpec((tm, GDN_V_W), row),
            pl.BlockSpec((tm, GDN_V_W), lambda i: (i, COL_Z)),
            pl.BlockSpec((tm, DIFF_W), row),
            pl.BlockSpec((tm, d), lambda i: (i, COL_GA * DIFF_W // d)),
            pl.BlockSpec((tm, d), lambda i: (i, COL_GB * DIFF_W // d)),
            pl.BlockSpec((None, 1, GDN_DV), lambda i: (layer, 0, 0)),
            pl.BlockSpec((None, 1, DIFF_DV), lambda i: (layer, 0, 0)),
            wspec(GDN_V_W, d), wspec(DIFF_W, d), wspec(d, d),
        ],
        out_specs=pl.BlockSpec((tm, d), row),
        scratch_shapes=[pltpu.VMEM((tm, GDN_V_W), bf16), pltpu.VMEM((tm, DIFF_W), bf16)],
        compiler_params=_params("parallel"),
        name="merge",
    )(x, o_g, p, o_d, p, p, gn, sn, wbg, wbd, wo)


def _final_body(x_ref, w_ref, o_ref):
    o_ref[...] = _rms(x_ref[...], w_ref[...])


def _final_norm(x, w):
    m, d = x.shape
    tm = _pick_tile(m, 768)
    return pl.pallas_call(
        _final_body,
        out_shape=jax.ShapeDtypeStruct((m, d), f32),
        grid=(m // tm,),
        in_specs=[pl.BlockSpec((tm, d), lambda i: (i, 0)), pl.BlockSpec((1, d), lambda i: (0, 0))],
        out_specs=pl.BlockSpec((tm, d), lambda i: (i, 0)),
        compiler_params=_params("parallel"),
        name="final_norm",
    )(x, w)


def kernel(x_prompt, x_sample, cache_k, cache_v, state_delta, state_conv, page_table, meta_tokens,
           ffn1_norm, ffn1_gate, ffn1_up, ffn1_down, mix_norm, w_in, conv_w, gdn_a_log, gdn_dt_bias,
           gdn_out_norm, diff_lambda, diff_subln, w_branch_gdn, w_branch_diff, w_out, ffn2_norm,
           ffn2_gate, ffn2_up, ffn2_down, final_norm):
    nb, seq, d = x_prompt.shape
    ns, ls, _ = x_sample.shape
    depth = w_in.shape[0]
    n_tok = seq + N_META
    rps = -(-n_tok // CHUNK) * CHUNK
    n_smp = ns * ls
    assert rps - n_tok >= n_smp and d == 2 * DIFF_W
    past = page_table.shape[1] * PAGE_SIZE
    n_pool = cache_k.shape[1]

    x = jnp.concatenate([jnp.broadcast_to(meta_tokens[None], (nb, N_META, d)), x_prompt,
                         jnp.zeros((nb, rps - n_tok, d), f32)], axis=1).reshape(nb * rps, d)
    s_lo, s_hi = n_tok, n_tok + n_smp
    x = x.at[s_lo:s_hi].set(x_sample.reshape(n_smp, d))

    pos = jnp.tile(jnp.arange(rps, dtype=jnp.int32), nb)
    pos = pos.at[s_lo:s_hi].set(jnp.tile(past + jnp.arange(ls, dtype=jnp.int32), ns))
    tabs = _rope_tables(pos)

    gate_lo = CONV_CH + GDN_V_W
    gate_hi = gate_lo + 2 * GDN_HEADS
    diff_hi = gate_hi + 3 * DIFF_W
    w_main = jnp.concatenate([w_in[:, :, :gate_lo], w_in[:, :, diff_hi:], w_in[:, :, gate_hi:diff_hi]],
                             axis=-1).astype(bf16)
    w_tail = jnp.pad(w_in[:, :, gate_lo:gate_hi], ((0, 0), (0, 0), (0, LANES - 2 * GDN_HEADS))).astype(bf16)
    cast = lambda w: w.astype(bf16)
    f1g, f1u, f1d, f2g, f2u, f2d = map(cast, (ffn1_gate, ffn1_up, ffn1_down, ffn2_gate, ffn2_up, ffn2_down))
    wbg, wbd, wo = map(cast, (w_branch_gdn, w_branch_diff, w_out))
    row3 = lambda w: w.reshape(depth, 1, -1)
    n1, n2, nm, gn, sn = map(row3, (ffn1_norm, ffn2_norm, mix_norm, gdn_out_norm, diff_subln))
    lane_pad = lambda w: jnp.pad(w, ((0, 0), (GDN_HEADS, LANES - 2 * GDN_HEADS))).reshape(depth, 1, LANES)
    alog_row, dtb_row = lane_pad(gdn_a_log), lane_pad(gdn_dt_bias)

    ck = jnp.transpose(cache_k, (0, 1, 3, 4, 5, 2)).reshape(depth, n_pool, DIFF_W, PAGE_SIZE)
    cv = cache_v.reshape(depth, n_pool, PAGE_SIZE * DIFF_HEADS, DIFF_DV)
    zero_state = jnp.zeros((nb, GDN_HEADS, GDN_DK, GDN_DV), f32)
    zero_conv = jnp.zeros((nb, SUBLANES, CONV_CH), f32)
    conv_pad = jnp.pad(state_conv, ((0, 0), (0, 0), (SUBLANES - (CONV_W - 1), 0), (0, 0)))

    def smp_chunks(a):
        w = a.shape[-1]
        return jnp.pad(a.reshape(ns, ls, w), ((0, 0), (0, CHUNK - ls), (0, 0))).reshape(ns * CHUNK, w)

    outs = [[] for _ in range(8)]
    for l in range(depth):
        lam_init = 0.8 - 0.6 * math.exp(-0.3 * l)
        x = _ffn(x, n1, f1g, f1u, f1d, l)
        p, tail = _inproj(x, nm, w_main, w_tail, l)
        q_bf, k_bf, v_bf, k_f = _rope(p, tabs)

        o_g, st_p = _gdn(p, tail, conv_w, alog_row, dtb_row, zero_state, zero_conv, l, nb, rps // CHUNK, n_tok)
        qkv_s = p[s_lo:s_hi, :CONV_CH]
        og_s, st_s = _gdn(smp_chunks(qkv_s), smp_chunks(tail[s_lo:s_hi]), conv_w, alog_row, dtb_row,
                          state_delta[l], conv_pad[l], l, ns, 1, ls)
        o_g = o_g.at[s_lo:s_hi].set(og_s.reshape(ns, CHUNK, GDN_V_W)[:, :ls].reshape(n_smp, GDN_V_W))

        o_d = _attn(diff_lambda, q_bf, k_bf, v_bf, l, lam_init, nb, rps)
        q_s = q_bf[s_lo:s_hi].reshape(ns, ls, DIFF_HEADS, 2, DIFF_DQK)
        hc = DIFF_HEADS * 2
        eye_hc = jnp.eye(hc, dtype=bf16).reshape(DIFF_HEADS, 2, 1, DIFF_HEADS, 2, 1)
        qt = (jnp.transpose(q_s, (0, 2, 3, 1, 4))[:, :, :, :, None, None, :]
              * eye_hc[None, :, :, :, :, :, :]).reshape(ns, hc * ls, DIFF_W)
        kt_new = jnp.pad(jnp.transpose(k_f[s_lo:s_hi].reshape(ns, ls, DIFF_W), (0, 2, 1)),
                         ((0, 0), (0, 0), (0, PAGE_SIZE - ls)))
        v_s = p[s_lo:s_hi, COL_DV * DIFF_W:(COL_DV + 1) * DIFF_W]
        v_new = jnp.pad(v_s.reshape(ns, ls, DIFF_HEADS, DIFF_DV),
                        ((0, 0), (0, PAGE_SIZE - ls), (0, 0), (0, 0))).reshape(ns, PAGE_SIZE * DIFF_HEADS, DIFF_DV)
        od_s = _decode(page_table, diff_lambda, qt, ck, cv, kt_new, v_new, l, lam_init, ls)
        o_d = o_d.at[s_lo:s_hi].set(od_s[:, :ls].reshape(n_smp, DIFF_W))

        x = _merge(x, o_g, p, o_d, gn, sn, wbg, wbd, wo, l, lam_init)
        x = _ffn(x, n2, f2g, f2u, f2d, l)

        seq_view = lambda a: a.reshape(nb, rps, a.shape[-1])
        outs[0].append(seq_view(k_f)[:, :n_tok].reshape(nb, n_tok, DIFF_HEADS, 2, DIFF_DQK))
        outs[1].append(seq_view(p[:, COL_DV * DIFF_W:(COL_DV + 1) * DIFF_W])[:, :n_tok]
                       .reshape(nb, n_tok, DIFF_HEADS, DIFF_DV))
        outs[2].append(st_p)
        outs[3].append(seq_view(p)[:, n_tok - (CONV_W - 1):n_tok, :CONV_CH])
        outs[4].append(k_f[s_lo:s_hi].reshape(ns, ls, DIFF_HEADS, 2, DIFF_DQK))
        outs[5].append(v_s.reshape(ns, ls, DIFF_HEADS, DIFF_DV))
        outs[6].append(st_s)
        outs[7].append(jnp.concatenate([state_conv[l], qkv_s.reshape(ns, ls, CONV_CH)], axis=1)[:, -(CONV_W - 1):])

    y = _final_norm(x, final_norm.reshape(1, d))
    y_prompt = y.reshape(nb, rps, d)[:, N_META:n_tok]
    y_sample = y[s_lo:s_hi].reshape(ns, ls, d)
    return (y_prompt, y_sample) + tuple(jnp.stack(o) for o in outs)
```

```python
import functools
import math

import jax
import jax.numpy as jnp
from jax import lax
from jax.experimental import pallas as pl
from jax.experimental.pallas import tpu as pltpu

N_META = 16
GDN_HEADS = 8
GDN_DK = 128
GDN_DV = 128
CONV_W = 4
DIFF_HEADS = 8
DIFF_DQK = 64
DIFF_DV = 128
ROPE_THETA = 500000.0
ROPE_DIM = DIFF_DQK // 4
PAGE_SIZE = 128
EPS = 1e-6

CHUNK = 128
LANES = 128
SUBLANES = 8
NEG = -1e30
VMEM_LIMIT = 56 * 1024 * 1024
PAGES_PER_STEP = 8

GDN_QK_W = GDN_HEADS * GDN_DK
GDN_V_W = GDN_HEADS * GDN_DV
CONV_CH = 2 * GDN_QK_W + GDN_V_W
DIFF_W = DIFF_HEADS * 2 * DIFF_DQK
COL_Z, COL_GA, COL_GB, COL_DQ, COL_DK, COL_DV = 3, 4, 6, 8, 9, 10

f32 = jnp.float32
bf16 = jnp.bfloat16


def _params(*sem):
    return pltpu.CompilerParams(dimension_semantics=sem, vmem_limit_bytes=VMEM_LIMIT)


def _pick_tile(n, target, mult=SUBLANES):
    best = None
    for t in range(mult, min(n, target) + 1, mult):
        if n % t == 0:
            best = t
    assert best is not None, (n, target, mult)
    return best


def _rms(x, w):
    return x * lax.rsqrt(jnp.mean(x * x, axis=-1, keepdims=True) + EPS) * w


def _dot(a, b):
    return jnp.dot(a.astype(bf16), b.astype(bf16), preferred_element_type=f32)


def _dot_nt(a, b):
    return lax.dot_general(a.astype(bf16), b.astype(bf16), (((1,), (1,)), ((), ())),
                           preferred_element_type=f32)


def _dot_exact_lhs(a_bf, x):
    x1 = x.astype(bf16)
    r1 = x - x1.astype(f32)
    x2 = r1.astype(bf16)
    x3 = (r1 - x2.astype(f32)).astype(bf16)
    d = functools.partial(jnp.dot, preferred_element_type=f32)
    return d(a_bf, x1) + (d(a_bf, x2) + d(a_bf, x3))


def _dot_exact_rhs(x, b_bf):
    x1 = x.astype(bf16)
    r1 = x - x1.astype(f32)
    x2 = r1.astype(bf16)
    x3 = (r1 - x2.astype(f32)).astype(bf16)
    d = functools.partial(jnp.dot, preferred_element_type=f32)
    return d(x1, b_bf) + (d(x2, b_bf) + d(x3, b_bf))


def _ffn_body(x_ref, nw_ref, wg_ref, wu_ref, wd_ref, o_ref, h_ref):
    j = pl.program_id(1)

    @pl.when(j == 0)
    def _():
        h_ref[...] = _rms(x_ref[...], nw_ref[...]).astype(bf16)
        o_ref[...] = jnp.zeros_like(o_ref)

    h = h_ref[...]
    g = jnp.dot(h, wg_ref[...], preferred_element_type=f32)
    u = jnp.dot(h, wu_ref[...], preferred_element_type=f32)
    a = (g * jax.nn.sigmoid(g) * u).astype(bf16)
    o_ref[...] += jnp.dot(a, wd_ref[...], preferred_element_type=f32)

    @pl.when(j == pl.num_programs(1) - 1)
    def _():
        o_ref[...] = x_ref[...] + 0.5 * o_ref[...]


def _ffn(x, nw, wg, wu, wd, layer):
    m, d = x.shape
    ff = wg.shape[-1]
    tm = _pick_tile(m, 768)
    tf = _pick_tile(ff, 512, LANES)
    return pl.pallas_call(
        _ffn_body,
        out_shape=jax.ShapeDtypeStruct((m, d), f32),
        grid=(m // tm, ff // tf),
        in_specs=[
            pl.BlockSpec((tm, d), lambda i, j: (i, 0)),
            pl.BlockSpec((None, 1, d), lambda i, j: (layer, 0, 0)),
            pl.BlockSpec((None, d, tf), lambda i, j: (layer, 0, j)),
            pl.BlockSpec((None, d, tf), lambda i, j: (layer, 0, j)),
            pl.BlockSpec((None, tf, d), lambda i, j: (layer, j, 0)),
        ],
        out_specs=pl.BlockSpec((tm, d), lambda i, j: (i, 0)),
        scratch_shapes=[pltpu.VMEM((tm, d), bf16)],
        compiler_params=_params("parallel", "arbitrary"),
        name="ffn",
    )(x, nw, wg, wu, wd)


def _inproj_body(x_ref, nw_ref, wa_ref, wb_ref, wt_ref, o_ref, t_ref, h_ref, *, na):
    j = pl.program_id(1)

    @pl.when(j == 0)
    def _():
        h = _rms(x_ref[...], nw_ref[...]).astype(bf16)
        h_ref[...] = h
        t_ref[...] = jnp.dot(h, wt_ref[...], preferred_element_type=f32)

    @pl.when(j < na)
    def _():
        o_ref[...] = jnp.dot(h_ref[...], wa_ref[...], preferred_element_type=f32)

    @pl.when(j >= na)
    def _():
        o_ref[...] = jnp.dot(h_ref[...], wb_ref[...], preferred_element_type=f32)


def _inproj(x, nw, w_a, w_b, w_tail, layer):
    m, d = x.shape
    tn = DIFF_W
    na, nb = w_a.shape[-1] // tn, w_b.shape[-1] // tn
    rot = 3
    assert na == COL_GA and na + nb - rot == COL_DQ
    tm = _pick_tile(m, 768)

    def b_index(i, j):
        return (layer, 0, jnp.where(j < na, rot, (j - na + rot) % nb))

    return pl.pallas_call(
        functools.partial(_inproj_body, na=na),
        out_shape=(jax.ShapeDtypeStruct((m, (na + nb) * tn), f32), jax.ShapeDtypeStruct((m, LANES), f32)),
        grid=(m // tm, na + nb),
        in_specs=[
            pl.BlockSpec((tm, d), lambda i, j: (i, 0)),
            pl.BlockSpec((None, 1, d), lambda i, j: (layer, 0, 0)),
            pl.BlockSpec((None, d, tn), lambda i, j: (layer, 0, jnp.minimum(j, na - 1))),
            pl.BlockSpec((None, d, tn), b_index),
            pl.BlockSpec((None, d, LANES), lambda i, j: (layer, 0, 0)),
        ],
        out_specs=(pl.BlockSpec((tm, tn), lambda i, j: (i, j)),
                   pl.BlockSpec((tm, LANES), lambda i, j: (i, 0))),
        scratch_shapes=[pltpu.VMEM((tm, d), bf16)],
        compiler_params=_params("parallel", "arbitrary"),
        name="inproj",
    )(x, nw, w_a, w_b, w_tail)


def _rope_body(dq_ref, dk_ref, dv_ref, c_ref, s1_ref, s2_ref, q_ref, kb_ref, vb_ref, kf_ref):
    c, s1, s2 = c_ref[...], s1_ref[...], s2_ref[...]
    for h in range(DIFF_HEADS):
        sl = slice(h * LANES, (h + 1) * LANES)
        for src, scale, dsts in ((dq_ref, DIFF_DQK ** -0.5, (q_ref,)), (dk_ref, None, (kb_ref, kf_ref))):
            x = src[:, sl]
            half = ROPE_DIM // 2
            r = x * c + pltpu.roll(x, half, 1) * s1 + pltpu.roll(x, LANES - half, 1) * s2
            if scale is not None:
                r = r * scale
            for dst in dsts:
                dst[:, sl] = r.astype(dst.dtype)
    vb_ref[...] = dv_ref[...].astype(bf16)


def _rope(p, tabs):
    m = p.shape[0]
    tr = _pick_tile(m, 384, 16)
    row = lambda i: (i, 0)
    col = lambda cb: pl.BlockSpec((tr, DIFF_W), lambda i: (i, cb))
    tab = pl.BlockSpec((tr, LANES), row)
    return pl.pallas_call(
        _rope_body,
        out_shape=(jax.ShapeDtypeStruct((m, DIFF_W), bf16),) * 3 + (jax.ShapeDtypeStruct((m, DIFF_W), f32),),
        grid=(m // tr,),
        in_specs=[col(COL_DQ), col(COL_DK), col(COL_DV), tab, tab, tab],
        out_specs=(pl.BlockSpec((tr, DIFF_W), row),) * 4,
        compiler_params=_params("parallel"),
        name="rope",
    )(p, p, p, *tabs)


def _rope_tables(pos):
    half = ROPE_DIM // 2
    inv = ROPE_THETA ** (-jnp.arange(half, dtype=f32) / half)
    ang = pos.astype(f32)[:, None] * inv[None, :]
    cos, sin = jnp.cos(ang), jnp.sin(ang)
    n = pos.shape[0]
    rest = DIFF_DQK - ROPE_DIM
    one, zero, zh = jnp.ones((n, rest), f32), jnp.zeros((n, rest), f32), jnp.zeros((n, half), f32)
    c = jnp.concatenate([cos, cos, one], axis=1)
    s1 = jnp.concatenate([zh, sin, zero], axis=1)
    s2 = jnp.concatenate([-sin, zh, zero], axis=1)
    rep = LANES // DIFF_DQK
    return tuple(jnp.tile(t, (1, rep)) for t in (c, s1, s2))


INV_BASE = 16


def _block_masks(row, colm):
    masks = [(row // INV_BASE) == (colm // INV_BASE)]
    nb = INV_BASE
    while nb < row.shape[0]:
        masks.append(((row // nb) == (colm // nb) + 1) & ((colm // nb) % 2 == 0))
        nb *= 2
    return masks


def _tri_inverse(a_list, eye, masks):
    n = a_list[0].shape[0]
    ad = [jnp.where(masks[0], a, 0.0) for a in a_list]
    t = [eye - x for x in ad]
    p = [_dot(x, x) for x in ad]
    steps = int(math.log2(INV_BASE)) - 1
    for s in range(steps):
        if s + 1 < steps:
            tp = [_dot(jnp.concatenate([ti, pi], axis=0), pi) for ti, pi in zip(t, p)]
            t = [ti + x[:n] for ti, x in zip(t, tp)]
            p = [x[n:] for x in tp]
        else:
            t = [ti + _dot(ti, pi) for ti, pi in zip(t, p)]
    for m in masks[1:]:
        w = [_dot(jnp.where(m, a, 0.0), ti) for a, ti in zip(a_list, t)]
        t = [ti - _dot(ti, wi) for ti, wi in zip(t, w)]
    return t


def _gdn_body(qkv_ref, tail_ref, cw_ref, alog_ref, dtb_ref, s0_ref, c0_ref, o_ref, so_ref,
              xp_ref, s_ref, *, n_valid):
    c = pl.program_id(1)
    L = CHUNK

    @pl.when(c == 0)
    def _():
        xp_ref[...] = c0_ref[...]
        s_ref[...] = s0_ref[...]

    u = qkv_ref[...]
    prev = xp_ref[...]
    cw = cw_ref[...]
    row8 = lax.broadcasted_iota(jnp.int32, (SUBLANES, CONV_CH), 0)
    y = u * cw[CONV_W - 1:CONV_W, :]
    for sh in range(1, CONV_W):
        us = pltpu.roll(u, sh, 0)
        top = jnp.where(row8 < sh, pltpu.roll(prev, sh, 0), us[0:SUBLANES, :])
        y = y + jnp.concatenate([top, us[SUBLANES:, :]], axis=0) * cw[CONV_W - 1 - sh:CONV_W - sh, :]
    xp_ref[...] = u[L - SUBLANES:, :]
    y = y * jax.nn.sigmoid(y)

    row = lax.broadcasted_iota(jnp.int32, (L, L), 0)
    colm = lax.broadcasted_iota(jnp.int32, (L, L), 1)
    incl = row >= colm
    strict = row > colm
    eye = jnp.where(row == colm, 1.0, 0.0).astype(f32)
    masks = _block_masks(row, colm)
    ltri = jnp.where(incl, 1.0, 0.0).astype(bf16)
    utri = jnp.where(row <= colm, 1.0, 0.0).astype(bf16)

    t = tail_ref[...]
    valid = (c * L + row) < n_valid
    beta = jnp.where(valid, jax.nn.sigmoid(t), 0.0)
    z = t + dtb_ref[...]
    softplus = jnp.maximum(z, 0.0) + jnp.log(1.0 + jnp.exp(-jnp.abs(z)))
    g = jnp.where(valid, -jnp.exp(alog_ref[...]) * softplus, 0.0)
    g_col = _dot_exact_lhs(ltri, g)
    g_row = _dot_exact_rhs(g.T, utri)
    gam = jnp.exp(g_col)
    g_last = g_col[L - 1:L, :]
    kdec = jnp.exp(g_last - g_col)
    s_scale = jnp.exp(g_last)

    heads = range(GDN_HEADS)
    lane = [GDN_HEADS + h for h in heads]
    l2 = lambda x: x * lax.rsqrt(jnp.sum(x * x, axis=-1, keepdims=True) + EPS)
    q = [l2(y[:, h * GDN_DK:(h + 1) * GDN_DK]) * (GDN_DK ** -0.5) for h in heads]
    k = [l2(y[:, GDN_QK_W + h * GDN_DK:GDN_QK_W + (h + 1) * GDN_DK]) for h in heads]
    v = [y[:, 2 * GDN_QK_W + h * GDN_DV:2 * GDN_QK_W + (h + 1) * GDN_DV] for h in heads]
    beta_c = [beta[:, h:h + 1] for h in heads]
    gam_c = [gam[:, gl:gl + 1] for gl in lane]
    dec = [jnp.exp(jnp.where(incl, g_col[:, gl:gl + 1] - g_row[gl:gl + 1, :], NEG)) for gl in lane]
    kb = [k[h] * beta_c[h] for h in heads]
    k_bf = [x.astype(bf16) for x in k]
    a = [_dot_nt(kb[h], k_bf[h]) * jnp.where(strict, dec[h], 0.0) for h in heads]
    qk = [_dot_nt(q[h], k_bf[h]) * dec[h] for h in heads]
    tinv = _tri_inverse(a, eye, masks)
    sol = [_dot(tinv[h], jnp.concatenate([kb[h] * gam_c[h], v[h] * beta_c[h]], axis=1)) for h in heads]
    s_old = [s_ref[h] for h in heads]
    u = [sol[h][:, GDN_DK:] - _dot(sol[h][:, :GDN_DK], s_old[h]) for h in heads]
    o = [_dot(q[h] * gam_c[h], s_old[h]) + _dot(qk[h], u[h]) for h in heads]
    for h in heads:
        o_ref[:, h * GDN_DV:(h + 1) * GDN_DV] = o[h]
    for h, gl in zip(heads, lane):
        kd = (k[h] * kdec[:, gl:gl + 1]).T
        s_ref[h] = s_scale[:, gl:gl + 1] * s_old[h] + _dot(kd, u[h])

    @pl.when(c == pl.num_programs(1) - 1)
    def _():
        so_ref[...] = s_ref[...]


def _gdn(qkv_src, tail, conv_w, alog_row, dtb_row, s0, c0, layer, n_seq, n_chunks, n_valid):
    rows = n_seq * n_chunks * CHUNK
    return pl.pallas_call(
        functools.partial(_gdn_body, n_valid=n_valid),
        out_shape=(jax.ShapeDtypeStruct((rows, GDN_V_W), f32),
                   jax.ShapeDtypeStruct((n_seq, GDN_HEADS, GDN_DK, GDN_DV), f32)),
        grid=(n_seq, n_chunks),
        in_specs=[
            pl.BlockSpec((CHUNK, CONV_CH), lambda s, c: (s * n_chunks + c, 0)),
            pl.BlockSpec((CHUNK, LANES), lambda s, c: (s * n_chunks + c, 0)),
            pl.BlockSpec((None, CONV_W, CONV_CH), lambda s, c: (layer, 0, 0)),
            pl.BlockSpec((None, 1, LANES), lambda s, c: (layer, 0, 0)),
            pl.BlockSpec((None, 1, LANES), lambda s, c: (layer, 0, 0)),
            pl.BlockSpec((None, GDN_HEADS, GDN_DK, GDN_DV), lambda s, c: (s, 0, 0, 0)),
            pl.BlockSpec((None, SUBLANES, CONV_CH), lambda s, c: (s, 0, 0)),
        ],
        out_specs=(pl.BlockSpec((CHUNK, GDN_V_W), lambda s, c: (s * n_chunks + c, 0)),
                   pl.BlockSpec((None, GDN_HEADS, GDN_DK, GDN_DV), lambda s, c: (s, 0, 0, 0))),
        scratch_shapes=[pltpu.VMEM((SUBLANES, CONV_CH), f32),
                        pltpu.VMEM((GDN_HEADS, GDN_DK, GDN_DV), f32)],
        compiler_params=_params("parallel", "arbitrary"),
        name="gdn",
    )(qkv_src, tail, conv_w, alog_row, dtb_row, s0, c0)


def _lambda(lv_ref, lam_init):
    lv = lv_ref[...]
    a = jnp.sum(lv[0:1, :] * lv[1:2, :], axis=-1, keepdims=True)
    b = jnp.sum(lv[2:3, :] * lv[3:4, :], axis=-1, keepdims=True)
    return jnp.exp(a) - jnp.exp(b) + lam_init


ATTN_ROW_TILE = 64


def _attn_body(lv_ref, q_ref, k_ref, v_ref, o_ref, q2_ref, s_ref, p_ref, m_ref, l_ref, al_ref, acc_ref,
               *, tq, lam_init):
    qi = pl.program_id(2)
    rt = ATTN_ROW_TILE
    q = q_ref[...].astype(f32)
    lane = lax.broadcasted_iota(jnp.int32, (tq, LANES), 1)
    q2_ref[0] = jnp.where(lane < DIFF_DQK, q, 0.0).astype(bf16)
    q2_ref[1] = jnp.where(lane >= DIFF_DQK, q, 0.0).astype(bf16)
    m_ref[...] = jnp.full_like(m_ref, NEG)
    l_ref[...] = jnp.zeros_like(l_ref)
    acc_ref[...] = jnp.zeros_like(acc_ref)

    def scores(kj, slot):
        kb = k_ref[pl.ds(pl.multiple_of(kj * tq, tq), tq), :]
        for c in range(2):
            s_ref[slot, c] = lax.dot_general(q2_ref[c], kb, (((1,), (1,)), ((), ())),
                                             preferred_element_type=f32)

    def softmax(slot, masked):
        tiles = [slice(r * rt, (r + 1) * rt) for r in range(tq // rt)]
        lanes = [slice(j * LANES, (j + 1) * LANES) for j in range(tq // LANES)]
        wide = lambda x: jnp.broadcast_to(x, (rt, LANES))
        for c in range(2):
            for rows in tiles:
                s = s_ref[slot, c, rows, :]
                if masked:
                    ri = rows.start + lax.broadcasted_iota(jnp.int32, (rt, tq), 0)
                    ci = lax.broadcasted_iota(jnp.int32, (rt, tq), 1)
                    s = jnp.where(ri >= ci, s, NEG)
                    s_ref[slot, c, rows, :] = s
                m_old = m_ref[c, rows, :]
                m_new = jnp.maximum(m_old, wide(jnp.max(s, axis=-1, keepdims=True)))
                m_ref[c, rows, :] = m_new
                al_ref[slot, c, rows, :] = jnp.exp(m_old - m_new)
        for c in range(2):
            for rows in tiles:
                m_t = m_ref[c, rows, :]
                ps = [jnp.exp(s_ref[slot, c, rows, ln] - m_t) for ln in lanes]
                for ln, pj in zip(lanes, ps):
                    p_ref[slot, c, rows, ln] = pj.astype(bf16)
                psum = wide(jnp.sum(functools.reduce(jnp.add, ps), axis=-1, keepdims=True))
                l_ref[c, rows, :] = al_ref[slot, c, rows, :] * l_ref[c, rows, :] + psum

    def pv(kj, slot):
        vb = v_ref[pl.ds(pl.multiple_of(kj * tq, tq), tq), :]
        for c in range(2):
            acc_ref[c] = acc_ref[c] * al_ref[slot, c] + jnp.dot(p_ref[slot, c], vb,
                                                                preferred_element_type=f32)

    def step(kj, slot, has_prev):
        scores(kj + 1, 1 - slot)
        if has_prev:
            pv(kj - 1, 1 - slot)
        softmax(slot, False)

    scores(0, 0)

    @pl.when(qi >= 1)
    def _():
        step(0, 0, False)

    def pair(t, carry):
        step(2 * t + 1, 1, True)
        step(2 * t + 2, 0, True)
        return carry

    rest = jnp.maximum(qi - 1, 0)
    lax.fori_loop(0, rest // 2, pair, 0)

    @pl.when(rest % 2 == 1)
    def _():
        step(qi - 1, 1, True)

    @pl.when(qi % 2 == 0)
    def _():
        @pl.when(qi >= 1)
        def _():
            pv(qi - 1, 1)
        softmax(0, True)
        pv(qi, 0)

    @pl.when(qi % 2 == 1)
    def _():
        pv(qi - 1, 0)
        softmax(1, True)
        pv(qi, 1)

    lam = _lambda(lv_ref, lam_init)
    o_ref[...] = acc_ref[0] / l_ref[0] - lam * (acc_ref[1] / l_ref[1])


def _attn(lam_vecs, q, k, v, layer, lam_init, n_batch, rows_per_seq):
    tq = _pick_tile(rows_per_seq, 384, CHUNK)
    nq = rows_per_seq // tq
    assert tq % ATTN_ROW_TILE == 0
    return pl.pallas_call(
        functools.partial(_attn_body, tq=tq, lam_init=lam_init),
        out_shape=jax.ShapeDtypeStruct((n_batch * rows_per_seq, DIFF_HEADS * DIFF_DV), f32),
        grid=(n_batch, DIFF_HEADS, nq),
        in_specs=[
            pl.BlockSpec((None, 4, DIFF_DQK), lambda b, h, i: (layer, 0, 0)),
            pl.BlockSpec((tq, LANES), lambda b, h, i: (b * nq + i, h)),
            pl.BlockSpec((rows_per_seq, LANES), lambda b, h, i: (b, h)),
            pl.BlockSpec((rows_per_seq, LANES), lambda b, h, i: (b, h)),
        ],
        out_specs=pl.BlockSpec((tq, DIFF_DV), lambda b, h, i: (b * nq + i, h)),
        scratch_shapes=[pltpu.VMEM((2, tq, LANES), bf16),
                        pltpu.VMEM((2, 2, tq, tq), f32),
                        pltpu.VMEM((2, 2, tq, tq), bf16),
                        pltpu.VMEM((2, tq, LANES), f32), pltpu.VMEM((2, tq, LANES), f32),
                        pltpu.VMEM((2, 2, tq, LANES), f32),
                        pltpu.VMEM((2, tq, DIFF_DV), f32)],
        compiler_params=_params("parallel", "parallel", "arbitrary"),
        name="diff_attn",
    )(lam_vecs, q, k, v)


def _decode_body(pt_ref, lv_ref, q_ref, *refs, n_pages_step, lam_init, n_new):
    del pt_ref
    k_refs = refs[:n_pages_step]
    v_refs = refs[n_pages_step:2 * n_pages_step]
    kn_ref, vn_ref, o_ref, m_ref, l_ref, acc_ref = refs[2 * n_pages_step:]
    step = pl.program_id(1)
    nq = q_ref.shape[0]

    @pl.when(step == 0)
    def _():
        m_ref[...] = jnp.full_like(m_ref, NEG)
        l_ref[...] = jnp.zeros_like(l_ref)
        acc_ref[...] = jnp.zeros_like(acc_ref)

    qt = q_ref[...]

    def process(ks, vs, mask):
        ss = [jnp.dot(qt, kr[...].astype(bf16), preferred_element_type=f32) for kr in ks]
        if mask is not None:
            ss = [jnp.where(mask, s, NEG) for s in ss]
        mx = functools.reduce(jnp.maximum, [jnp.max(s, axis=1, keepdims=True) for s in ss])
        m_old = m_ref[...]
        m_new = jnp.maximum(m_old, mx)
        alpha = jnp.exp(m_old - m_new)
        ps = [jnp.exp(s - m_new) for s in ss]
        l_ref[...] = alpha * l_ref[...] + functools.reduce(
            jnp.add, [jnp.sum(p, axis=1, keepdims=True) for p in ps])
        m_ref[...] = m_new
        for h in range(DIFF_HEADS):
            rows = slice(h * SUBLANES, (h + 1) * SUBLANES)
            acc = acc_ref[h] * alpha[rows]
            for p, vr in zip(ps, vs):
                vh = vr[pl.ds(h, PAGE_SIZE, stride=DIFF_HEADS), :].astype(bf16)
                acc = acc + jnp.dot(p[rows].astype(bf16), vh, preferred_element_type=f32)
            acc_ref[h] = acc

    process(k_refs, v_refs, None)

    @pl.when(step == pl.num_programs(1) - 1)
    def _():
        tok = lax.broadcasted_iota(jnp.int32, (nq, PAGE_SIZE), 0) % n_new
        key = lax.broadcasted_iota(jnp.int32, (nq, PAGE_SIZE), 1)
        process([kn_ref], [vn_ref], key <= tok)
        lam = _lambda(lv_ref, lam_init)
        linv = 1.0 / l_ref[...]
        for h in range(DIFF_HEADS):
            a = acc_ref[h] * linv[h * SUBLANES:(h + 1) * SUBLANES]
            o_ref[:, h * DIFF_DV:(h + 1) * DIFF_DV] = a - lam * pltpu.roll(a, n_new, 0)


def _decode(page_table, lam_vecs, qt, cache_kt, cache_v, kt_new, v_new, layer, lam_init, n_new):
    n_seq, n_pages = page_table.shape
    g = PAGES_PER_STEP
    nq = qt.shape[1]
    assert n_pages % g == 0 and 2 * n_new == SUBLANES and nq == DIFF_HEADS * SUBLANES
    kr, vr = cache_kt.shape[2], cache_v.shape[2]

    def page_spec(rows, u):
        return pl.BlockSpec((None, None, rows, LANES),
                            lambda b, s, pt: (layer, pt[b * n_pages + s * g + u], 0, 0))

    new_spec = lambda rows: pl.BlockSpec((None, rows, LANES), lambda b, s, pt: (b, 0, 0))
    grid_spec = pltpu.PrefetchScalarGridSpec(
        num_scalar_prefetch=1,
        grid=(n_seq, n_pages // g),
        in_specs=[pl.BlockSpec((None, 4, DIFF_DQK), lambda b, s, pt: (layer, 0, 0)),
                  pl.BlockSpec((None, nq, kr), lambda b, s, pt: (b, 0, 0))]
                 + [page_spec(kr, u) for u in range(g)] + [page_spec(vr, u) for u in range(g)]
                 + [new_spec(kr), new_spec(vr)],
        out_specs=pl.BlockSpec((None, SUBLANES, DIFF_HEADS * DIFF_DV), lambda b, s, pt: (b, 0, 0)),
        scratch_shapes=[pltpu.VMEM((nq, 1), f32), pltpu.VMEM((nq, 1), f32),
                        pltpu.VMEM((DIFF_HEADS, SUBLANES, DIFF_DV), f32)],
    )
    return pl.pallas_call(
        functools.partial(_decode_body, n_pages_step=g, lam_init=lam_init, n_new=n_new),
        out_shape=jax.ShapeDtypeStruct((n_seq, SUBLANES, DIFF_HEADS * DIFF_DV), f32),
        grid_spec=grid_spec,
        compiler_params=_params("parallel", "arbitrary"),
        name="decode_attn",
    )(page_table.reshape(-1), lam_vecs, qt, *([cache_kt] * g), *([cache_v] * g), kt_new, v_new)


def _merge_body(x_ref, og_ref, z_ref, od_ref, ga_ref, gb_ref, gn_ref, sn_ref, wbg_ref, wbd_ref,
                wo_ref, o_ref, yg_ref, yd_ref, *, d_scale):
    gn, sn = gn_ref[...], sn_ref[...]
    for h in range(GDN_HEADS):
        sl = slice(h * GDN_DV, (h + 1) * GDN_DV)
        zz = z_ref[:, sl]
        yg_ref[:, sl] = (_rms(og_ref[:, sl], gn) * (zz * jax.nn.sigmoid(zz))).astype(bf16)
    for h in range(DIFF_HEADS):
        sl = slice(h * DIFF_DV, (h + 1) * DIFF_DV)
        yd_ref[:, sl] = (_rms(od_ref[:, sl], sn) * d_scale).astype(bf16)
    mix = (jax.nn.sigmoid(ga_ref[...]) * jnp.dot(yg_ref[...], wbg_ref[...], preferred_element_type=f32)
           + jax.nn.sigmoid(gb_ref[...]) * jnp.dot(yd_ref[...], wbd_ref[...], preferred_element_type=f32))
    o_ref[...] = x_ref[...] + jnp.dot(mix.astype(bf16), wo_ref[...], preferred_element_type=f32)


def _merge(x, o_g, p, o_d, gn, sn, wbg, wbd, wo, layer, lam_init):
    m, d = x.shape
    tm = _pick_tile(m, 256)
    row = lambda i: (i, 0)
    wspec = lambda r, c: pl.BlockSpec((None, r, c), lambda i: (layer, 0, 0), pipeline_mode=pl.Buffered(1))
    return pl.pallas_call(
        functools.partial(_merge_body, d_scale=1.0 - lam_init),
        out_shape=jax.ShapeDtypeStruct((m, d), f32),
        grid=(m // tm,),
        in_specs=[
            pl.BlockSpec((tm, d), row),
            pl.BlockSpec((tm, GDN_V_W), row),
            pl.BlockSpec((tm, GDN_V_W), lambda i: (i, COL_Z)),
            pl.BlockSpec((tm, DIFF_W), row),
            pl.BlockSpec((tm, d), lambda i: (i, COL_GA * DIFF_W // d)),
            pl.BlockSpec((tm, d), lambda i: (i, COL_GB * DIFF_W // d)),
            pl.BlockSpec((None, 1, GDN_DV), lambda i: (layer, 0, 0)),
            pl.BlockSpec((None, 1, DIFF_DV), lambda i: (layer, 0, 0)),
            wspec(GDN_V_W, d), wspec(DIFF_W, d), wspec(d, d),
        ],
        out_specs=pl.BlockSpec((tm, d), row),
        scratch_shapes=[pltpu.VMEM((tm, GDN_V_W), bf16), pltpu.VMEM((tm, DIFF_W), bf16)],
        compiler_params=_params("parallel"),
        name="merge",
    )(x, o_g, p, o_d, p, p, gn, sn, wbg, wbd, wo)


def _final_body(x_ref, w_ref, o_ref):
    o_ref[...] = _rms(x_ref[...], w_ref[...])


def _final_norm(x, w):
    m, d = x.shape
    tm = _pick_tile(m, 768)
    return pl.pallas_call(
        _final_body,
        out_shape=jax.ShapeDtypeStruct((m, d), f32),
        grid=(m // tm,),
        in_specs=[pl.BlockSpec((tm, d), lambda i: (i, 0)), pl.BlockSpec((1, d), lambda i: (0, 0))],
        out_specs=pl.BlockSpec((tm, d), lambda i: (i, 0)),
        compiler_params=_params("parallel"),
        name="final_norm",
    )(x, w)


def kernel(x_prompt, x_sample, cache_k, cache_v, state_delta, state_conv, page_table, meta_tokens,
           ffn1_norm, ffn1_gate, ffn1_up, ffn1_down, mix_norm, w_in, conv_w, gdn_a_log, gdn_dt_bias,
           gdn_out_norm, diff_lambda, diff_subln, w_branch_gdn, w_branch_diff, w_out, ffn2_norm,
           ffn2_gate, ffn2_up, ffn2_down, final_norm):
    nb, seq, d = x_prompt.shape
    ns, ls, _ = x_sample.shape
    depth = w_in.shape[0]
    n_tok = seq + N_META
    rps = -(-n_tok // CHUNK) * CHUNK
    n_smp = ns * ls
    assert rps - n_tok >= n_smp and d == 2 * DIFF_W
    past = page_table.shape[1] * PAGE_SIZE
    n_pool = cache_k.shape[1]

    x = jnp.concatenate([jnp.broadcast_to(meta_tokens[None], (nb, N_META, d)), x_prompt,
                         jnp.zeros((nb, rps - n_tok, d), f32)], axis=1).reshape(nb * rps, d)
    s_lo, s_hi = n_tok, n_tok + n_smp
    x = x.at[s_lo:s_hi].set(x_sample.reshape(n_smp, d))

    pos = jnp.tile(jnp.arange(rps, dtype=jnp.int32), nb)
    pos = pos.at[s_lo:s_hi].set(jnp.tile(past + jnp.arange(ls, dtype=jnp.int32), ns))
    tabs = _rope_tables(pos)

    gate_lo = CONV_CH + GDN_V_W
    gate_hi = gate_lo + 2 * GDN_HEADS
    w_a = w_in[:, :, :gate_lo].astype(bf16)
    w_b = w_in[:, :, gate_hi:].astype(bf16)
    w_tail = jnp.pad(w_in[:, :, gate_lo:gate_hi], ((0, 0), (0, 0), (0, LANES - 2 * GDN_HEADS))).astype(bf16)
    cast = lambda w: w.astype(bf16)
    f1g, f1u, f1d, f2g, f2u, f2d = map(cast, (ffn1_gate, ffn1_up, ffn1_down, ffn2_gate, ffn2_up, ffn2_down))
    wbg, wbd, wo = map(cast, (w_branch_gdn, w_branch_diff, w_out))
    row3 = lambda w: w.reshape(depth, 1, -1)
    n1, n2, nm, gn, sn = map(row3, (ffn1_norm, ffn2_norm, mix_norm, gdn_out_norm, diff_subln))
    lane_pad = lambda w: jnp.pad(w, ((0, 0), (GDN_HEADS, LANES - 2 * GDN_HEADS))).reshape(depth, 1, LANES)
    alog_row, dtb_row = lane_pad(gdn_a_log), lane_pad(gdn_dt_bias)

    ck = jnp.transpose(cache_k, (0, 1, 3, 4, 5, 2)).reshape(depth, n_pool, DIFF_W, PAGE_SIZE)
    cv = cache_v.reshape(depth, n_pool, PAGE_SIZE * DIFF_HEADS, DIFF_DV)
    zero_state = jnp.zeros((nb, GDN_HEADS, GDN_DK, GDN_DV), f32)
    zero_conv = jnp.zeros((nb, SUBLANES, CONV_CH), f32)
    conv_pad = jnp.pad(state_conv, ((0, 0), (0, 0), (SUBLANES - (CONV_W - 1), 0), (0, 0)))

    def smp_chunks(a):
        w = a.shape[-1]
        return jnp.pad(a.reshape(ns, ls, w), ((0, 0), (0, CHUNK - ls), (0, 0))).reshape(ns * CHUNK, w)

    outs = [[] for _ in range(8)]
    for l in range(depth):
        lam_init = 0.8 - 0.6 * math.exp(-0.3 * l)
        x = _ffn(x, n1, f1g, f1u, f1d, l)
        p, tail = _inproj(x, nm, w_a, w_b, w_tail, l)
        q_bf, k_bf, v_bf, k_f = _rope(p, tabs)

        o_g, st_p = _gdn(p, tail, conv_w, alog_row, dtb_row, zero_state, zero_conv, l, nb, rps // CHUNK, n_tok)
        qkv_s = p[s_lo:s_hi, :CONV_CH]
        og_s, st_s = _gdn(smp_chunks(qkv_s), smp_chunks(tail[s_lo:s_hi]), conv_w, alog_row, dtb_row,
                          state_delta[l], conv_pad[l], l, ns, 1, ls)
        o_g = o_g.at[s_lo:s_hi].set(og_s.reshape(ns, CHUNK, GDN_V_W)[:, :ls].reshape(n_smp, GDN_V_W))

        o_d = _attn(diff_lambda, q_bf, k_bf, v_bf, l, lam_init, nb, rps)
        q_s = q_bf[s_lo:s_hi].reshape(ns, ls, DIFF_HEADS, 2, DIFF_DQK)
        hc = DIFF_HEADS * 2
        eye_hc = jnp.eye(hc, dtype=bf16).reshape(DIFF_HEADS, 2, 1, DIFF_HEADS, 2, 1)
        qt = (jnp.transpose(q_s, (0, 2, 3, 1, 4))[:, :, :, :, None, None, :]
              * eye_hc[None, :, :, :, :, :, :]).reshape(ns, hc * ls, DIFF_W)
        kt_new = jnp.pad(jnp.transpose(k_f[s_lo:s_hi].reshape(ns, ls, DIFF_W), (0, 2, 1)),
                         ((0, 0), (0, 0), (0, PAGE_SIZE - ls)))
        v_s = p[s_lo:s_hi, COL_DV * DIFF_W:(COL_DV + 1) * DIFF_W]
        v_new = jnp.pad(v_s.reshape(ns, ls, DIFF_HEADS, DIFF_DV),
                        ((0, 0), (0, PAGE_SIZE - ls), (0, 0), (0, 0))).reshape(ns, PAGE_SIZE * DIFF_HEADS, DIFF_DV)
        od_s = _decode(page_table, diff_lambda, qt, ck, cv, kt_new, v_new, l, lam_init, ls)
        o_d = o_d.at[s_lo:s_hi].set(od_s[:, :ls].reshape(n_smp, DIFF_W))

        x = _merge(x, o_g, p, o_d, gn, sn, wbg, wbd, wo, l, lam_init)
        x = _ffn(x, n2, f2g, f2u, f2d, l)

        seq_view = lambda a: a.reshape(nb, rps, a.shape[-1])
        outs[0].append(seq_view(k_f)[:, :n_tok].reshape(nb, n_tok, DIFF_HEADS, 2, DIFF_DQK))
        outs[1].append(seq_view(p[:, COL_DV * DIFF_W:(COL_DV + 1) * DIFF_W])[:, :n_tok]
                       .reshape(nb, n_tok, DIFF_HEADS, DIFF_DV))
        outs[2].append(st_p)
        outs[3].append(seq_view(p)[:, n_tok - (CONV_W - 1):n_tok, :CONV_CH])
        outs[4].append(k_f[s_lo:s_hi].reshape(ns, ls, DIFF_HEADS, 2, DIFF_DQK))
        outs[5].append(v_s.reshape(ns, ls, DIFF_HEADS, DIFF_DV))
        outs[6].append(st_s)
        outs[7].append(jnp.concatenate([state_conv[l], qkv_s.reshape(ns, ls, CONV_CH)], axis=1)[:, -(CONV_W - 1):])

    y = _final_norm(x, final_norm.reshape(1, d))
    y_prompt = y.reshape(nb, rps, d)[:, N_META:n_tok]
    y_sample = y[s_lo:s_hi].reshape(ns, ls, d)
    return (y_prompt, y_sample) + tuple(jnp.stack(o) for o in outs)
```

```python
import functools
import math

import jax
import jax.numpy as jnp
from jax import lax
from jax.experimental import pallas as pl
from jax.experimental.pallas import tpu as pltpu

N_META = 16
GDN_HEADS = 8
GDN_DK = 128
GDN_DV = 128
CONV_W = 4
DIFF_HEADS = 8
DIFF_DQK = 64
DIFF_DV = 128
ROPE_THETA = 500000.0
ROPE_DIM = DIFF_DQK // 4
PAGE_SIZE = 128
EPS = 1e-6

CHUNK = 128
LANES = 128
SUBLANES = 8
NEG = -1e30
VMEM_LIMIT = 56 * 1024 * 1024
PAGES_PER_STEP = 8

GDN_QK_W = GDN_HEADS * GDN_DK
GDN_V_W = GDN_HEADS * GDN_DV
CONV_CH = 2 * GDN_QK_W + GDN_V_W
DIFF_W = DIFF_HEADS * 2 * DIFF_DQK
COL_Z, COL_GA, COL_GB, COL_DQ, COL_DK, COL_DV = 3, 4, 6, 8, 9, 10

f32 = jnp.float32
bf16 = jnp.bfloat16


def _params(*sem):
    return pltpu.CompilerParams(dimension_semantics=sem, vmem_limit_bytes=VMEM_LIMIT)


def _pick_tile(n, target, mult=SUBLANES):
    best = None
    for t in range(mult, min(n, target) + 1, mult):
        if n % t == 0:
            best = t
    assert best is not None, (n, target, mult)
    return best


def _rms(x, w):
    return x * lax.rsqrt(jnp.mean(x * x, axis=-1, keepdims=True) + EPS) * w


def _dot(a, b):
    return jnp.dot(a.astype(bf16), b.astype(bf16), preferred_element_type=f32)


def _dot_nt(a, b):
    return lax.dot_general(a.astype(bf16), b.astype(bf16), (((1,), (1,)), ((), ())),
                           preferred_element_type=f32)


def _dot_exact_lhs(a_bf, x):
    x1 = x.astype(bf16)
    r1 = x - x1.astype(f32)
    x2 = r1.astype(bf16)
    x3 = (r1 - x2.astype(f32)).astype(bf16)
    d = functools.partial(jnp.dot, preferred_element_type=f32)
    return d(a_bf, x1) + (d(a_bf, x2) + d(a_bf, x3))


def _dot_exact_rhs(x, b_bf):
    x1 = x.astype(bf16)
    r1 = x - x1.astype(f32)
    x2 = r1.astype(bf16)
    x3 = (r1 - x2.astype(f32)).astype(bf16)
    d = functools.partial(jnp.dot, preferred_element_type=f32)
    return d(x1, b_bf) + (d(x2, b_bf) + d(x3, b_bf))


def _ffn_body(x_ref, nw_ref, wg_ref, wu_ref, wd_ref, o_ref, h_ref):
    j = pl.program_id(1)

    @pl.when(j == 0)
    def _():
        h_ref[...] = _rms(x_ref[...], nw_ref[...]).astype(bf16)
        o_ref[...] = jnp.zeros_like(o_ref)

    h = h_ref[...]
    g = jnp.dot(h, wg_ref[...], preferred_element_type=f32)
    u = jnp.dot(h, wu_ref[...], preferred_element_type=f32)
    a = (g * jax.nn.sigmoid(g) * u).astype(bf16)
    o_ref[...] += jnp.dot(a, wd_ref[...], preferred_element_type=f32)

    @pl.when(j == pl.num_programs(1) - 1)
    def _():
        o_ref[...] = x_ref[...] + 0.5 * o_ref[...]


def _ffn(x, nw, wg, wu, wd, layer):
    m, d = x.shape
    ff = wg.shape[-1]
    tm = _pick_tile(m, 768)
    tf = _pick_tile(ff, 512, LANES)
    return pl.pallas_call(
        _ffn_body,
        out_shape=jax.ShapeDtypeStruct((m, d), f32),
        grid=(m // tm, ff // tf),
        in_specs=[
            pl.BlockSpec((tm, d), lambda i, j: (i, 0)),
            pl.BlockSpec((None, 1, d), lambda i, j: (layer, 0, 0)),
            pl.BlockSpec((None, d, tf), lambda i, j: (layer, 0, j)),
            pl.BlockSpec((None, d, tf), lambda i, j: (layer, 0, j)),
            pl.BlockSpec((None, tf, d), lambda i, j: (layer, j, 0)),
        ],
        out_specs=pl.BlockSpec((tm, d), lambda i, j: (i, 0)),
        scratch_shapes=[pltpu.VMEM((tm, d), bf16)],
        compiler_params=_params("parallel", "arbitrary"),
        name="ffn",
    )(x, nw, wg, wu, wd)


def _inproj_body(x_ref, nw_ref, w_ref, wt_ref, o_ref, t_ref, h_ref):
    @pl.when(pl.program_id(1) == 0)
    def _():
        h = _rms(x_ref[...], nw_ref[...]).astype(bf16)
        h_ref[...] = h
        t_ref[...] = jnp.dot(h, wt_ref[...], preferred_element_type=f32)

    o_ref[...] = jnp.dot(h_ref[...], w_ref[...], preferred_element_type=f32)


def _inproj(x, nw, w_main, w_tail, layer):
    m, d = x.shape
    n = w_main.shape[-1]
    tm = _pick_tile(m, 768)
    tn = DIFF_W
    return pl.pallas_call(
        _inproj_body,
        out_shape=(jax.ShapeDtypeStruct((m, n), f32), jax.ShapeDtypeStruct((m, LANES), f32)),
        grid=(m // tm, n // tn),
        in_specs=[
            pl.BlockSpec((tm, d), lambda i, j: (i, 0)),
            pl.BlockSpec((None, 1, d), lambda i, j: (layer, 0, 0)),
            pl.BlockSpec((None, d, tn), lambda i, j: (layer, 0, j)),
            pl.BlockSpec((None, d, LANES), lambda i, j: (layer, 0, 0)),
        ],
        out_specs=(pl.BlockSpec((tm, tn), lambda i, j: (i, j)),
                   pl.BlockSpec((tm, LANES), lambda i, j: (i, 0))),
        scratch_shapes=[pltpu.VMEM((tm, d), bf16)],
        compiler_params=_params("parallel", "arbitrary"),
        name="inproj",
    )(x, nw, w_main, w_tail)


def _rope_body(dq_ref, dk_ref, dv_ref, c_ref, s1_ref, s2_ref, q_ref, kb_ref, vb_ref, kf_ref):
    c, s1, s2 = c_ref[...], s1_ref[...], s2_ref[...]
    for h in range(DIFF_HEADS):
        sl = slice(h * LANES, (h + 1) * LANES)
        for src, scale, dsts in ((dq_ref, DIFF_DQK ** -0.5, (q_ref,)), (dk_ref, None, (kb_ref, kf_ref))):
            x = src[:, sl]
            half = ROPE_DIM // 2
            r = x * c + pltpu.roll(x, half, 1) * s1 + pltpu.roll(x, LANES - half, 1) * s2
            if scale is not None:
                r = r * scale
            for dst in dsts:
                dst[:, sl] = r.astype(dst.dtype)
    vb_ref[...] = dv_ref[...].astype(bf16)


def _rope(p, tabs):
    m = p.shape[0]
    tr = _pick_tile(m, 384, 16)
    row = lambda i: (i, 0)
    col = lambda cb: pl.BlockSpec((tr, DIFF_W), lambda i: (i, cb))
    tab = pl.BlockSpec((tr, LANES), row)
    return pl.pallas_call(
        _rope_body,
        out_shape=(jax.ShapeDtypeStruct((m, DIFF_W), bf16),) * 3 + (jax.ShapeDtypeStruct((m, DIFF_W), f32),),
        grid=(m // tr,),
        in_specs=[col(COL_DQ), col(COL_DK), col(COL_DV), tab, tab, tab],
        out_specs=(pl.BlockSpec((tr, DIFF_W), row),) * 4,
        compiler_params=_params("parallel"),
        name="rope",
    )(p, p, p, *tabs)


def _rope_tables(pos):
    half = ROPE_DIM // 2
    inv = ROPE_THETA ** (-jnp.arange(half, dtype=f32) / half)
    ang = pos.astype(f32)[:, None] * inv[None, :]
    cos, sin = jnp.cos(ang), jnp.sin(ang)
    n = pos.shape[0]
    rest = DIFF_DQK - ROPE_DIM
    one, zero, zh = jnp.ones((n, rest), f32), jnp.zeros((n, rest), f32), jnp.zeros((n, half), f32)
    c = jnp.concatenate([cos, cos, one], axis=1)
    s1 = jnp.concatenate([zh, sin, zero], axis=1)
    s2 = jnp.concatenate([-sin, zh, zero], axis=1)
    rep = LANES // DIFF_DQK
    return tuple(jnp.tile(t, (1, rep)) for t in (c, s1, s2))


INV_BASE = 16


def _block_masks(row, colm):
    masks = [(row // INV_BASE) == (colm // INV_BASE)]
    nb = INV_BASE
    while nb < row.shape[0]:
        masks.append(((row // nb) == (colm // nb) + 1) & ((colm // nb) % 2 == 0))
        nb *= 2
    return masks


def _tri_inverse(a_list, eye, masks):
    n = a_list[0].shape[0]
    ad = [jnp.where(masks[0], a, 0.0) for a in a_list]
    t = [eye - x for x in ad]
    p = [_dot(x, x) for x in ad]
    steps = int(math.log2(INV_BASE)) - 1
    for s in range(steps):
        if s + 1 < steps:
            tp = [_dot(jnp.concatenate([ti, pi], axis=0), pi) for ti, pi in zip(t, p)]
            t = [ti + x[:n] for ti, x in zip(t, tp)]
            p = [x[n:] for x in tp]
        else:
            t = [ti + _dot(ti, pi) for ti, pi in zip(t, p)]
    for m in masks[1:]:
        w = [_dot(jnp.where(m, a, 0.0), ti) for a, ti in zip(a_list, t)]
        t = [ti - _dot(ti, wi) for ti, wi in zip(t, w)]
    return t


def _gdn_body(qkv_ref, tail_ref, cw_ref, alog_ref, dtb_ref, s0_ref, c0_ref, o_ref, so_ref,
              xp_ref, s_ref, *, n_valid):
    c = pl.program_id(1)
    L = CHUNK

    @pl.when(c == 0)
    def _():
        xp_ref[...] = c0_ref[...]
        s_ref[...] = s0_ref[...]

    row = lax.broadcasted_iota(jnp.int32, (L, L), 0)
    colm = lax.broadcasted_iota(jnp.int32, (L, L), 1)
    incl = row >= colm
    strict = row > colm
    eye = jnp.where(row == colm, 1.0, 0.0).astype(f32)
    masks = _block_masks(row, colm)
    ltri = jnp.where(incl, 1.0, 0.0).astype(bf16)
    utri = jnp.where(row <= colm, 1.0, 0.0).astype(bf16)
    valid = (c * L + row) < n_valid
    cw = cw_ref[...]
    row8 = lax.broadcasted_iota(jnp.int32, (SUBLANES, CONV_CH), 0)
    l2 = lambda x: x * lax.rsqrt(jnp.sum(x * x, axis=-1, keepdims=True) + EPS)

    idx, q, k, v, beta_c, gam_c, dec, kdec_c, scale_c = [], [], [], [], [], [], [], [], []
    for sq in range(qkv_ref.shape[0]):
        u = qkv_ref[sq]
        prev = xp_ref[sq]
        y = u * cw[CONV_W - 1:CONV_W, :]
        for sh in range(1, CONV_W):
            us = pltpu.roll(u, sh, 0)
            top = jnp.where(row8 < sh, pltpu.roll(prev, sh, 0), us[0:SUBLANES, :])
            y = y + jnp.concatenate([top, us[SUBLANES:, :]], axis=0) * cw[CONV_W - 1 - sh:CONV_W - sh, :]
        xp_ref[sq] = u[L - SUBLANES:, :]
        y = y * jax.nn.sigmoid(y)

        t = tail_ref[sq]
        beta = jnp.where(valid, jax.nn.sigmoid(t), 0.0)
        z = t + dtb_ref[...]
        softplus = jnp.maximum(z, 0.0) + jnp.log(1.0 + jnp.exp(-jnp.abs(z)))
        g = jnp.where(valid, -jnp.exp(alog_ref[...]) * softplus, 0.0)
        g_col = _dot_exact_lhs(ltri, g)
        g_row = _dot_exact_rhs(g.T, utri)
        gam = jnp.exp(g_col)
        g_last = g_col[L - 1:L, :]
        kdec = jnp.exp(g_last - g_col)
        s_scale = jnp.exp(g_last)
        for h in range(GDN_HEADS):
            gl = GDN_HEADS + h
            idx.append((sq, h))
            q.append(l2(y[:, h * GDN_DK:(h + 1) * GDN_DK]) * (GDN_DK ** -0.5))
            k.append(l2(y[:, GDN_QK_W + h * GDN_DK:GDN_QK_W + (h + 1) * GDN_DK]))
            v.append(y[:, 2 * GDN_QK_W + h * GDN_DV:2 * GDN_QK_W + (h + 1) * GDN_DV])
            beta_c.append(beta[:, h:h + 1])
            gam_c.append(gam[:, gl:gl + 1])
            dec.append(jnp.exp(jnp.where(incl, g_col[:, gl:gl + 1] - g_row[gl:gl + 1, :], NEG)))
            kdec_c.append(kdec[:, gl:gl + 1])
            scale_c.append(s_scale[:, gl:gl + 1])

    ch = range(len(idx))
    kb = [k[i] * beta_c[i] for i in ch]
    k_bf = [x.astype(bf16) for x in k]
    a = [_dot_nt(kb[i], k_bf[i]) * jnp.where(strict, dec[i], 0.0) for i in ch]
    qk = [_dot_nt(q[i], k_bf[i]) * dec[i] for i in ch]
    tinv = _tri_inverse(a, eye, masks)
    sol = [_dot(tinv[i], jnp.concatenate([kb[i] * gam_c[i], v[i] * beta_c[i]], axis=1)) for i in ch]
    s_old = [s_ref[sq, h] for sq, h in idx]
    u = [sol[i][:, GDN_DK:] - _dot(sol[i][:, :GDN_DK], s_old[i]) for i in ch]
    o = [_dot(q[i] * gam_c[i], s_old[i]) + _dot(qk[i], u[i]) for i in ch]
    for i, (sq, h) in enumerate(idx):
        o_ref[sq, :, h * GDN_DV:(h + 1) * GDN_DV] = o[i]
    for i, (sq, h) in enumerate(idx):
        kd = (k[i] * kdec_c[i]).T
        s_ref[sq, h] = scale_c[i] * s_old[i] + _dot(kd, u[i])

    @pl.when(c == pl.num_programs(1) - 1)
    def _():
        so_ref[...] = s_ref[...]


GDN_SEQS_PER_STEP = 2


def _gdn(qkv_src, tail, conv_w, alog_row, dtb_row, s0, c0, layer, n_valid):
    n_seq, rows, _ = qkv_src.shape
    g = GDN_SEQS_PER_STEP
    assert n_seq % g == 0 and rows % CHUNK == 0
    state = pl.BlockSpec((g, GDN_HEADS, GDN_DK, GDN_DV), lambda s, c: (s, 0, 0, 0))
    return pl.pallas_call(
        functools.partial(_gdn_body, n_valid=n_valid),
        out_shape=(jax.ShapeDtypeStruct((n_seq, rows, GDN_V_W), f32),
                   jax.ShapeDtypeStruct((n_seq, GDN_HEADS, GDN_DK, GDN_DV), f32)),
        grid=(n_seq // g, rows // CHUNK),
        in_specs=[
            pl.BlockSpec((g, CHUNK, CONV_CH), lambda s, c: (s, c, 0)),
            pl.BlockSpec((g, CHUNK, LANES), lambda s, c: (s, c, 0)),
            pl.BlockSpec((None, CONV_W, CONV_CH), lambda s, c: (layer, 0, 0)),
            pl.BlockSpec((None, 1, LANES), lambda s, c: (layer, 0, 0)),
            pl.BlockSpec((None, 1, LANES), lambda s, c: (layer, 0, 0)),
            state,
            pl.BlockSpec((g, SUBLANES, CONV_CH), lambda s, c: (s, 0, 0)),
        ],
        out_specs=(pl.BlockSpec((g, CHUNK, GDN_V_W), lambda s, c: (s, c, 0)), state),
        scratch_shapes=[pltpu.VMEM((g, SUBLANES, CONV_CH), f32),
                        pltpu.VMEM((g, GDN_HEADS, GDN_DK, GDN_DV), f32)],
        compiler_params=_params("parallel", "arbitrary"),
        name="gdn",
    )(qkv_src, tail, conv_w, alog_row, dtb_row, s0, c0)


def _lambda(lv_ref, lam_init):
    lv = lv_ref[...]
    a = jnp.sum(lv[0:1, :] * lv[1:2, :], axis=-1, keepdims=True)
    b = jnp.sum(lv[2:3, :] * lv[3:4, :], axis=-1, keepdims=True)
    return jnp.exp(a) - jnp.exp(b) + lam_init


ATTN_ROW_TILE = 64
ATTN_HEADS_PER_STEP = 4


def _attn_body(lv_ref, q_ref, k_ref, v_ref, o_ref, q2_ref, s_ref, p_ref, m_ref, l_ref, al_ref, acc_ref,
               *, tq, lam_init):
    qi = pl.program_id(2)
    rt = ATTN_ROW_TILE
    hp = ATTN_HEADS_PER_STEP
    units = range(2 * hp)
    head_lanes = [slice(h * LANES, (h + 1) * LANES) for h in range(hp)]
    lane = lax.broadcasted_iota(jnp.int32, (tq, LANES), 1)
    for h in range(hp):
        q = q_ref[:, head_lanes[h]].astype(f32)
        q2_ref[2 * h] = jnp.where(lane < DIFF_DQK, q, 0.0).astype(bf16)
        q2_ref[2 * h + 1] = jnp.where(lane >= DIFF_DQK, q, 0.0).astype(bf16)
    m_ref[...] = jnp.full_like(m_ref, NEG)
    l_ref[...] = jnp.zeros_like(l_ref)
    acc_ref[...] = jnp.zeros_like(acc_ref)

    def scores(kj, slot):
        start = pl.multiple_of(kj * tq, tq)
        for u in units:
            kb = k_ref[pl.ds(start, tq), head_lanes[u // 2]]
            s_ref[slot, u] = lax.dot_general(q2_ref[u], kb, (((1,), (1,)), ((), ())),
                                             preferred_element_type=f32)

    def softmax_pv(kj, slot, masked):
        start = pl.multiple_of(kj * tq, tq)
        tiles = [slice(r * rt, (r + 1) * rt) for r in range(tq // rt)]
        lanes = [slice(j * LANES, (j + 1) * LANES) for j in range(tq // LANES)]
        wide = lambda x: jnp.broadcast_to(x, (rt, LANES))
        for u in units:
            for rows in tiles:
                s = s_ref[slot, u, rows, :]
                if masked:
                    ri = rows.start + lax.broadcasted_iota(jnp.int32, (rt, tq), 0)
                    ci = lax.broadcasted_iota(jnp.int32, (rt, tq), 1)
                    s = jnp.where(ri >= ci, s, NEG)
                    s_ref[slot, u, rows, :] = s
                m_old = m_ref[u, rows, :]
                m_new = jnp.maximum(m_old, wide(jnp.max(s, axis=-1, keepdims=True)))
                m_ref[u, rows, :] = m_new
                al_ref[u, rows, :] = jnp.exp(m_old - m_new)
        for u in units:
            for rows in tiles:
                m_t = m_ref[u, rows, :]
                ps = [jnp.exp(s_ref[slot, u, rows, ln] - m_t) for ln in lanes]
                for ln, pj in zip(lanes, ps):
                    p_ref[u, rows, ln] = pj.astype(bf16)
                psum = wide(jnp.sum(functools.reduce(jnp.add, ps), axis=-1, keepdims=True))
                l_ref[u, rows, :] = al_ref[u, rows, :] * l_ref[u, rows, :] + psum
            vb = v_ref[pl.ds(start, tq), head_lanes[u // 2]]
            acc_ref[u] = acc_ref[u] * al_ref[u] + jnp.dot(p_ref[u], vb, preferred_element_type=f32)

    scores(0, 0)

    def block(kj, slot):
        scores(kj + 1, 1 - slot)
        softmax_pv(kj, slot, False)

    def pair(j2, carry):
        block(2 * j2, 0)
        block(2 * j2 + 1, 1)
        return carry

    lax.fori_loop(0, qi // 2, pair, 0)
    odd = qi % 2 == 1

    @pl.when(odd)
    def _():
        block(qi - 1, 0)
        softmax_pv(qi, 1, True)

    @pl.when(jnp.logical_not(odd))
    def _():
        softmax_pv(qi, 0, True)

    lam = _lambda(lv_ref, lam_init)
    for h in range(hp):
        o_ref[:, head_lanes[h]] = (acc_ref[2 * h] / l_ref[2 * h]
                                   - lam * (acc_ref[2 * h + 1] / l_ref[2 * h + 1]))


def _attn(lam_vecs, q, k, v, layer, lam_init, n_batch, rows_per_seq):
    tq = _pick_tile(rows_per_seq, 384, CHUNK)
    nq = rows_per_seq // tq
    hp = ATTN_HEADS_PER_STEP
    nu = 2 * hp
    assert tq % ATTN_ROW_TILE == 0 and DIFF_HEADS % hp == 0
    return pl.pallas_call(
        functools.partial(_attn_body, tq=tq, lam_init=lam_init),
        out_shape=jax.ShapeDtypeStruct((n_batch * rows_per_seq, DIFF_HEADS * DIFF_DV), f32),
        grid=(n_batch, DIFF_HEADS // hp, nq),
        in_specs=[
            pl.BlockSpec((None, 4, DIFF_DQK), lambda b, h, i: (layer, 0, 0)),
            pl.BlockSpec((tq, hp * LANES), lambda b, h, i: (b * nq + i, h)),
            pl.BlockSpec((rows_per_seq, hp * LANES), lambda b, h, i: (b, h)),
            pl.BlockSpec((rows_per_seq, hp * LANES), lambda b, h, i: (b, h)),
        ],
        out_specs=pl.BlockSpec((tq, hp * DIFF_DV), lambda b, h, i: (b * nq + i, h)),
        scratch_shapes=[pltpu.VMEM((nu, tq, LANES), bf16),
                        pltpu.VMEM((2, nu, tq, tq), f32),
                        pltpu.VMEM((nu, tq, tq), bf16),
                        pltpu.VMEM((nu, tq, LANES), f32), pltpu.VMEM((nu, tq, LANES), f32),
                        pltpu.VMEM((nu, tq, LANES), f32),
                        pltpu.VMEM((nu, tq, DIFF_DV), f32)],
        compiler_params=_params("parallel", "parallel", "arbitrary"),
        name="diff_attn",
    )(lam_vecs, q, k, v)


def _decode_body(pt_ref, lv_ref, q_ref, *refs, n_pages_step, lam_init, n_new):
    del pt_ref
    k_refs = refs[:n_pages_step]
    v_refs = refs[n_pages_step:2 * n_pages_step]
    kn_ref, vn_ref, o_ref, m_ref, l_ref, acc_ref = refs[2 * n_pages_step:]
    step = pl.program_id(1)
    nq = q_ref.shape[0]

    @pl.when(step == 0)
    def _():
        m_ref[...] = jnp.full_like(m_ref, NEG)
        l_ref[...] = jnp.zeros_like(l_ref)
        acc_ref[...] = jnp.zeros_like(acc_ref)

    qt = q_ref[...]

    def process(ks, vs, mask):
        ss = [jnp.dot(qt, kr[...].astype(bf16), preferred_element_type=f32) for kr in ks]
        if mask is not None:
            ss = [jnp.where(mask, s, NEG) for s in ss]
        mx = functools.reduce(jnp.maximum, [jnp.max(s, axis=1, keepdims=True) for s in ss])
        m_old = m_ref[...]
        m_new = jnp.maximum(m_old, mx)
        alpha = jnp.exp(m_old - m_new)
        ps = [jnp.exp(s - m_new) for s in ss]
        l_ref[...] = alpha * l_ref[...] + functools.reduce(
            jnp.add, [jnp.sum(p, axis=1, keepdims=True) for p in ps])
        m_ref[...] = m_new
        for h in range(DIFF_HEADS):
            rows = slice(h * SUBLANES, (h + 1) * SUBLANES)
            acc = acc_ref[h] * alpha[rows]
            for p, vr in zip(ps, vs):
                vh = vr[pl.ds(h, PAGE_SIZE, stride=DIFF_HEADS), :].astype(bf16)
                acc = acc + jnp.dot(p[rows].astype(bf16), vh, preferred_element_type=f32)
            acc_ref[h] = acc

    process(k_refs, v_refs, None)

    @pl.when(step == pl.num_programs(1) - 1)
    def _():
        tok = lax.broadcasted_iota(jnp.int32, (nq, PAGE_SIZE), 0) % n_new
        key = lax.broadcasted_iota(jnp.int32, (nq, PAGE_SIZE), 1)
        process([kn_ref], [vn_ref], key <= tok)
        lam = _lambda(lv_ref, lam_init)
        linv = 1.0 / l_ref[...]
        for h in range(DIFF_HEADS):
            a = acc_ref[h] * linv[h * SUBLANES:(h + 1) * SUBLANES]
            o_ref[:, h * DIFF_DV:(h + 1) * DIFF_DV] = a - lam * pltpu.roll(a, n_new, 0)


def _decode(page_table, lam_vecs, qt, cache_kt, cache_v, kt_new, v_new, layer, lam_init, n_new):
    n_seq, n_pages = page_table.shape
    g = PAGES_PER_STEP
    nq = qt.shape[1]
    assert n_pages % g == 0 and 2 * n_new == SUBLANES and nq == DIFF_HEADS * SUBLANES
    kr, vr = cache_kt.shape[2], cache_v.shape[2]

    def page_spec(rows, u):
        return pl.BlockSpec((None, None, rows, LANES),
                            lambda b, s, pt: (layer, pt[b * n_pages + s * g + u], 0, 0))

    new_spec = lambda rows: pl.BlockSpec((None, rows, LANES), lambda b, s, pt: (b, 0, 0))
    grid_spec = pltpu.PrefetchScalarGridSpec(
        num_scalar_prefetch=1,
        grid=(n_seq, n_pages // g),
        in_specs=[pl.BlockSpec((None, 4, DIFF_DQK), lambda b, s, pt: (layer, 0, 0)),
                  pl.BlockSpec((None, nq, kr), lambda b, s, pt: (b, 0, 0))]
                 + [page_spec(kr, u) for u in range(g)] + [page_spec(vr, u) for u in range(g)]
                 + [new_spec(kr), new_spec(vr)],
        out_specs=pl.BlockSpec((None, SUBLANES, DIFF_HEADS * DIFF_DV), lambda b, s, pt: (b, 0, 0)),
        scratch_shapes=[pltpu.VMEM((nq, 1), f32), pltpu.VMEM((nq, 1), f32),
                        pltpu.VMEM((DIFF_HEADS, SUBLANES, DIFF_DV), f32)],
    )
    return pl.pallas_call(
        functools.partial(_decode_body, n_pages_step=g, lam_init=lam_init, n_new=n_new),
        out_shape=jax.ShapeDtypeStruct((n_seq, SUBLANES, DIFF_HEADS * DIFF_DV), f32),
        grid_spec=grid_spec,
        compiler_params=_params("parallel", "arbitrary"),
        name="decode_attn",
    )(page_table.reshape(-1), lam_vecs, qt, *([cache_kt] * g), *([cache_v] * g), kt_new, v_new)


def _merge_body(x_ref, og_ref, z_ref, od_ref, ga_ref, gb_ref, gn_ref, sn_ref, wbg_ref, wbd_ref,
                wo_ref, o_ref, yg_ref, yd_ref, *, d_scale):
    gn, sn = gn_ref[...], sn_ref[...]
    for h in range(GDN_HEADS):
        sl = slice(h * GDN_DV, (h + 1) * GDN_DV)
        zz = z_ref[:, sl]
        yg_ref[:, sl] = (_rms(og_ref[:, sl], gn) * (zz * jax.nn.sigmoid(zz))).astype(bf16)
    for h in range(DIFF_HEADS):
        sl = slice(h * DIFF_DV, (h + 1) * DIFF_DV)
        yd_ref[:, sl] = (_rms(od_ref[:, sl], sn) * d_scale).astype(bf16)
    mix = (jax.nn.sigmoid(ga_ref[...]) * jnp.dot(yg_ref[...], wbg_ref[...], preferred_element_type=f32)
           + jax.nn.sigmoid(gb_ref[...]) * jnp.dot(yd_ref[...], wbd_ref[...], preferred_element_type=f32))
    o_ref[...] = x_ref[...] + jnp.dot(mix.astype(bf16), wo_ref[...], preferred_element_type=f32)


def _merge(x, o_g, p, o_d, gn, sn, wbg, wbd, wo, layer, lam_init):
    m, d = x.shape
    tm = _pick_tile(m, 256)
    row = lambda i: (i, 0)
    wspec = lambda r, c: pl.BlockSpec((None, r, c), lambda i: (layer, 0, 0), pipeline_mode=pl.Buffered(1))
    return pl.pallas_call(
        functools.partial(_merge_body, d_scale=1.0 - lam_init),
        out_shape=jax.ShapeDtypeStruct((m, d), f32),
        grid=(m // tm,),
        in_specs=[
            pl.BlockSpec((tm, d), row),
            pl.BlockSpec((tm, GDN_V_W), row),
            pl.BlockSpec((tm, GDN_V_W), lambda i: (i, COL_Z)),
            pl.BlockSpec((tm, DIFF_W), row),
            pl.BlockSpec((tm, d), lambda i: (i, COL_GA * DIFF_W // d)),
            pl.BlockSpec((tm, d), lambda i: (i, COL_GB * DIFF_W // d)),
            pl.BlockSpec((None, 1, GDN_DV), lambda i: (layer, 0, 0)),
            pl.BlockSpec((None, 1, DIFF_DV), lambda i: (layer, 0, 0)),
            wspec(GDN_V_W, d), wspec(DIFF_W, d), wspec(d, d),
        ],
        out_specs=pl.BlockSpec((tm, d), row),
        scratch_shapes=[pltpu.VMEM((tm, GDN_V_W), bf16), pltpu.VMEM((tm, DIFF_W), bf16)],
        compiler_params=_params("parallel"),
        name="merge",
    )(x, o_g, p, o_d, p, p, gn, sn, wbg, wbd, wo)


def _final_body(x_ref, w_ref, o_ref):
    o_ref[...] = _rms(x_ref[...], w_ref[...])


def _final_norm(x, w):
    m, d = x.shape
    tm = _pick_tile(m, 768)
    return pl.pallas_call(
        _final_body,
        out_shape=jax.ShapeDtypeStruct((m, d), f32),
        grid=(m // tm,),
        in_specs=[pl.BlockSpec((tm, d), lambda i: (i, 0)), pl.BlockSpec((1, d), lambda i: (0, 0))],
        out_specs=pl.BlockSpec((tm, d), lambda i: (i, 0)),
        compiler_params=_params("parallel"),
        name="final_norm",
    )(x, w)


def kernel(x_prompt, x_sample, cache_k, cache_v, state_delta, state_conv, page_table, meta_tokens,
           ffn1_norm, ffn1_gate, ffn1_up, ffn1_down, mix_norm, w_in, conv_w, gdn_a_log, gdn_dt_bias,
           gdn_out_norm, diff_lambda, diff_subln, w_branch_gdn, w_branch_diff, w_out, ffn2_norm,
           ffn2_gate, ffn2_up, ffn2_down, final_norm):
    nb, seq, d = x_prompt.shape
    ns, ls, _ = x_sample.shape
    depth = w_in.shape[0]
    n_tok = seq + N_META
    rps = -(-n_tok // CHUNK) * CHUNK
    n_smp = ns * ls
    assert rps - n_tok >= n_smp and d == 2 * DIFF_W
    past = page_table.shape[1] * PAGE_SIZE
    n_pool = cache_k.shape[1]

    x = jnp.concatenate([jnp.broadcast_to(meta_tokens[None], (nb, N_META, d)), x_prompt,
                         jnp.zeros((nb, rps - n_tok, d), f32)], axis=1).reshape(nb * rps, d)
    s_lo, s_hi = n_tok, n_tok + n_smp
    x = x.at[s_lo:s_hi].set(x_sample.reshape(n_smp, d))

    pos = jnp.tile(jnp.arange(rps, dtype=jnp.int32), nb)
    pos = pos.at[s_lo:s_hi].set(jnp.tile(past + jnp.arange(ls, dtype=jnp.int32), ns))
    tabs = _rope_tables(pos)

    gate_lo = CONV_CH + GDN_V_W
    gate_hi = gate_lo + 2 * GDN_HEADS
    diff_hi = gate_hi + 3 * DIFF_W
    w_main = jnp.concatenate([w_in[:, :, :gate_lo], w_in[:, :, diff_hi:], w_in[:, :, gate_hi:diff_hi]],
                             axis=-1).astype(bf16)
    w_tail = jnp.pad(w_in[:, :, gate_lo:gate_hi], ((0, 0), (0, 0), (0, LANES - 2 * GDN_HEADS))).astype(bf16)
    cast = lambda w: w.astype(bf16)
    f1g, f1u, f1d, f2g, f2u, f2d = map(cast, (ffn1_gate, ffn1_up, ffn1_down, ffn2_gate, ffn2_up, ffn2_down))
    wbg, wbd, wo = map(cast, (w_branch_gdn, w_branch_diff, w_out))
    row3 = lambda w: w.reshape(depth, 1, -1)
    n1, n2, nm, gn, sn = map(row3, (ffn1_norm, ffn2_norm, mix_norm, gdn_out_norm, diff_subln))
    lane_pad = lambda w: jnp.pad(w, ((0, 0), (GDN_HEADS, LANES - 2 * GDN_HEADS))).reshape(depth, 1, LANES)
    alog_row, dtb_row = lane_pad(gdn_a_log), lane_pad(gdn_dt_bias)

    ck = jnp.transpose(cache_k, (0, 1, 3, 4, 5, 2)).reshape(depth, n_pool, DIFF_W, PAGE_SIZE)
    cv = cache_v.reshape(depth, n_pool, PAGE_SIZE * DIFF_HEADS, DIFF_DV)
    zero_state = jnp.zeros((nb, GDN_HEADS, GDN_DK, GDN_DV), f32)
    zero_conv = jnp.zeros((nb, SUBLANES, CONV_CH), f32)
    conv_pad = jnp.pad(state_conv, ((0, 0), (0, 0), (SUBLANES - (CONV_W - 1), 0), (0, 0)))

    def smp_chunks(a):
        w = a.shape[-1]
        return jnp.pad(a.reshape(ns, ls, w), ((0, 0), (0, CHUNK - ls), (0, 0))).reshape(ns * CHUNK, w)

    outs = [[] for _ in range(8)]
    for l in range(depth):
        lam_init = 0.8 - 0.6 * math.exp(-0.3 * l)
        x = _ffn(x, n1, f1g, f1u, f1d, l)
        p, tail = _inproj(x, nm, w_main, w_tail, l)
        q_bf, k_bf, v_bf, k_f = _rope(p, tabs)

        o_g, st_p = _gdn(p.reshape(nb, rps, -1), tail.reshape(nb, rps, LANES), conv_w, alog_row, dtb_row,
                         zero_state, zero_conv, l, n_tok)
        qkv_s = p[s_lo:s_hi, :CONV_CH]
        og_s, st_s = _gdn(smp_chunks(qkv_s).reshape(ns, CHUNK, CONV_CH),
                          smp_chunks(tail[s_lo:s_hi]).reshape(ns, CHUNK, LANES), conv_w, alog_row, dtb_row,
                          state_delta[l], conv_pad[l], l, ls)
        o_g = o_g.reshape(nb * rps, GDN_V_W).at[s_lo:s_hi].set(og_s[:, :ls].reshape(n_smp, GDN_V_W))

        o_d = _attn(diff_lambda, q_bf, k_bf, v_bf, l, lam_init, nb, rps)
        q_s = q_bf[s_lo:s_hi].reshape(ns, ls, DIFF_HEADS, 2, DIFF_DQK)
        hc = DIFF_HEADS * 2
        eye_hc = jnp.eye(hc, dtype=bf16).reshape(DIFF_HEADS, 2, 1, DIFF_HEADS, 2, 1)
        qt = (jnp.transpose(q_s, (0, 2, 3, 1, 4))[:, :, :, :, None, None, :]
              * eye_hc[None, :, :, :, :, :, :]).reshape(ns, hc * ls, DIFF_W)
        kt_new = jnp.pad(jnp.transpose(k_f[s_lo:s_hi].reshape(ns, ls, DIFF_W), (0, 2, 1)),
                         ((0, 0), (0, 0), (0, PAGE_SIZE - ls)))
        v_s = p[s_lo:s_hi, COL_DV * DIFF_W:(COL_DV + 1) * DIFF_W]
        v_new = jnp.pad(v_s.reshape(ns, ls, DIFF_HEADS, DIFF_DV),
                        ((0, 0), (0, PAGE_SIZE - ls), (0, 0), (0, 0))).reshape(ns, PAGE_SIZE * DIFF_HEADS, DIFF_DV)
        od_s = _decode(page_table, diff_lambda, qt, ck, cv, kt_new, v_new, l, lam_init, ls)
        o_d = o_d.at[s_lo:s_hi].set(od_s[:, :ls].reshape(n_smp, DIFF_W))

        x = _merge(x, o_g, p, o_d, gn, sn, wbg, wbd, wo, l, lam_init)
        x = _ffn(x, n2, f2g, f2u, f2d, l)

        seq_view = lambda a: a.reshape(nb, rps, a.shape[-1])
        outs[0].append(seq_view(k_f)[:, :n_tok].reshape(nb, n_tok, DIFF_HEADS, 2, DIFF_DQK))
        outs[1].append(seq_view(p[:, COL_DV * DIFF_W:(COL_DV + 1) * DIFF_W])[:, :n_tok]
                       .reshape(nb, n_tok, DIFF_HEADS, DIFF_DV))
        outs[2].append(st_p)
        outs[3].append(seq_view(p)[:, n_tok - (CONV_W - 1):n_tok, :CONV_CH])
        outs[4].append(k_f[s_lo:s_hi].reshape(ns, ls, DIFF_HEADS, 2, DIFF_DQK))
        outs[5].append(v_s.reshape(ns, ls, DIFF_HEADS, DIFF_DV))
        outs[6].append(st_s)
        outs[7].append(jnp.concatenate([state_conv[l], qkv_s.reshape(ns, ls, CONV_CH)], axis=1)[:, -(CONV_W - 1):])

    y = _final_norm(x, final_norm.reshape(1, d))
    y_prompt = y.reshape(nb, rps, d)[:, N_META:n_tok]
    y_sample = y[s_lo:s_hi].reshape(ns, ls, d)
    return (y_prompt, y_sample) + tuple(jnp.stack(o) for o in outs)
```

```python
import functools
import math

import jax
import jax.numpy as jnp
from jax import lax
from jax.experimental import pallas as pl
from jax.experimental.pallas import tpu as pltpu

N_META = 16
GDN_HEADS = 8
GDN_DK = 128
GDN_DV = 128
CONV_W = 4
DIFF_HEADS = 8
DIFF_DQK = 64
DIFF_DV = 128
ROPE_THETA = 500000.0
ROPE_DIM = DIFF_DQK // 4
PAGE_SIZE = 128
EPS = 1e-6

CHUNK = 128
LANES = 128
SUBLANES = 8
NEG = -1e30
VMEM_LIMIT = 56 * 1024 * 1024
PAGES_PER_STEP = 8

GDN_QK_W = GDN_HEADS * GDN_DK
GDN_V_W = GDN_HEADS * GDN_DV
CONV_CH = 2 * GDN_QK_W + GDN_V_W
DIFF_W = DIFF_HEADS * 2 * DIFF_DQK
COL_Z, COL_GA, COL_GB, COL_DQ, COL_DK, COL_DV = 3, 4, 6, 8, 9, 10

f32 = jnp.float32
bf16 = jnp.bfloat16


def _params(*sem):
    return pltpu.CompilerParams(dimension_semantics=sem, vmem_limit_bytes=VMEM_LIMIT)


def _pick_tile(n, target, mult=SUBLANES):
    best = None
    for t in range(mult, min(n, target) + 1, mult):
        if n % t == 0:
            best = t
    assert best is not None, (n, target, mult)
    return best


def _rms(x, w):
    return x * lax.rsqrt(jnp.mean(x * x, axis=-1, keepdims=True) + EPS) * w


def _dot(a, b):
    return jnp.dot(a.astype(bf16), b.astype(bf16), preferred_element_type=f32)


def _dot_nt(a, b):
    return lax.dot_general(a.astype(bf16), b.astype(bf16), (((1,), (1,)), ((), ())),
                           preferred_element_type=f32)


def _dot_exact_lhs(a_bf, x):
    x1 = x.astype(bf16)
    r1 = x - x1.astype(f32)
    x2 = r1.astype(bf16)
    x3 = (r1 - x2.astype(f32)).astype(bf16)
    d = functools.partial(jnp.dot, preferred_element_type=f32)
    return d(a_bf, x1) + (d(a_bf, x2) + d(a_bf, x3))


def _dot_exact_rhs(x, b_bf):
    x1 = x.astype(bf16)
    r1 = x - x1.astype(f32)
    x2 = r1.astype(bf16)
    x3 = (r1 - x2.astype(f32)).astype(bf16)
    d = functools.partial(jnp.dot, preferred_element_type=f32)
    return d(x1, b_bf) + (d(x2, b_bf) + d(x3, b_bf))


def _ffn_body(x_ref, nw_ref, wg_ref, wu_ref, wd_ref, o_ref, h_ref):
    j = pl.program_id(1)

    @pl.when(j == 0)
    def _():
        h_ref[...] = _rms(x_ref[...], nw_ref[...]).astype(bf16)
        o_ref[...] = jnp.zeros_like(o_ref)

    h = h_ref[...]
    g = jnp.dot(h, wg_ref[...], preferred_element_type=f32)
    u = jnp.dot(h, wu_ref[...], preferred_element_type=f32)
    a = (g * jax.nn.sigmoid(g) * u).astype(bf16)
    o_ref[...] += jnp.dot(a, wd_ref[...], preferred_element_type=f32)

    @pl.when(j == pl.num_programs(1) - 1)
    def _():
        o_ref[...] = x_ref[...] + 0.5 * o_ref[...]


def _ffn(x, nw, wg, wu, wd, layer):
    m, d = x.shape
    ff = wg.shape[-1]
    tm = _pick_tile(m, 768)
    tf = _pick_tile(ff, 512, LANES)
    return pl.pallas_call(
        _ffn_body,
        out_shape=jax.ShapeDtypeStruct((m, d), f32),
        grid=(m // tm, ff // tf),
        in_specs=[
            pl.BlockSpec((tm, d), lambda i, j: (i, 0)),
            pl.BlockSpec((None, 1, d), lambda i, j: (layer, 0, 0)),
            pl.BlockSpec((None, d, tf), lambda i, j: (layer, 0, j)),
            pl.BlockSpec((None, d, tf), lambda i, j: (layer, 0, j)),
            pl.BlockSpec((None, tf, d), lambda i, j: (layer, j, 0)),
        ],
        out_specs=pl.BlockSpec((tm, d), lambda i, j: (i, 0)),
        scratch_shapes=[pltpu.VMEM((tm, d), bf16)],
        compiler_params=_params("parallel", "arbitrary"),
        name="ffn",
    )(x, nw, wg, wu, wd)


def _inproj_body(x_ref, nw_ref, w_ref, wt_ref, o_ref, t_ref, h_ref):
    @pl.when(pl.program_id(1) == 0)
    def _():
        h = _rms(x_ref[...], nw_ref[...]).astype(bf16)
        h_ref[...] = h
        t_ref[...] = _dot_nt(h, wt_ref[...])

    o_ref[...] = _dot_nt(h_ref[...], w_ref[0])


def _inproj(x, nw, w_t, w_tail, layer):
    m, d = x.shape
    tn = DIFF_W
    gate_lo = CONV_CH + GDN_V_W
    gate_hi = gate_lo + 2 * GDN_HEADS
    n = w_t.shape[1] - 2 * GDN_HEADS
    tm = _pick_tile(m, 768)

    def row_start(j):
        moved = jnp.where(j < COL_DQ, j + (COL_DV + 1 - COL_DQ), j - (COL_DQ - COL_GA))
        return pl.multiple_of(jnp.where(j < COL_GA, j * tn, gate_hi + (moved - COL_GA) * tn), 2 * SUBLANES)

    return pl.pallas_call(
        _inproj_body,
        out_shape=(jax.ShapeDtypeStruct((m, n), f32), jax.ShapeDtypeStruct((m, LANES), f32)),
        grid=(m // tm, n // tn),
        in_specs=[
            pl.BlockSpec((tm, d), lambda i, j: (i, 0)),
            pl.BlockSpec((None, 1, d), lambda i, j: (layer, 0, 0)),
            pl.BlockSpec((pl.Element(1), pl.Element(tn), pl.Element(d)), lambda i, j: (layer, row_start(j), 0)),
            pl.BlockSpec((None, LANES, d), lambda i, j: (layer, 0, 0)),
        ],
        out_specs=(pl.BlockSpec((tm, tn), lambda i, j: (i, j)),
                   pl.BlockSpec((tm, LANES), lambda i, j: (i, 0))),
        scratch_shapes=[pltpu.VMEM((tm, d), bf16)],
        compiler_params=_params("parallel", "arbitrary"),
        name="inproj",
    )(x, nw, w_t, w_tail)


def _rope_body(dq_ref, dk_ref, dv_ref, c_ref, s1_ref, s2_ref, q_ref, kb_ref, vb_ref, kf_ref):
    c, s1, s2 = c_ref[...], s1_ref[...], s2_ref[...]
    for h in range(DIFF_HEADS):
        sl = slice(h * LANES, (h + 1) * LANES)
        for src, scale, dsts in ((dq_ref, DIFF_DQK ** -0.5, (q_ref,)), (dk_ref, None, (kb_ref, kf_ref))):
            x = src[:, sl]
            half = ROPE_DIM // 2
            r = x * c + pltpu.roll(x, half, 1) * s1 + pltpu.roll(x, LANES - half, 1) * s2
            if scale is not None:
                r = r * scale
            for dst in dsts:
                dst[:, sl] = r.astype(dst.dtype)
    vb_ref[...] = dv_ref[...].astype(bf16)


def _rope(p, tabs):
    m = p.shape[0]
    tr = _pick_tile(m, 384, 16)
    row = lambda i: (i, 0)
    col = lambda cb: pl.BlockSpec((tr, DIFF_W), lambda i: (i, cb))
    tab = pl.BlockSpec((tr, LANES), row)
    return pl.pallas_call(
        _rope_body,
        out_shape=(jax.ShapeDtypeStruct((m, DIFF_W), bf16),) * 3 + (jax.ShapeDtypeStruct((m, DIFF_W), f32),),
        grid=(m // tr,),
        in_specs=[col(COL_DQ), col(COL_DK), col(COL_DV), tab, tab, tab],
        out_specs=(pl.BlockSpec((tr, DIFF_W), row),) * 4,
        compiler_params=_params("parallel"),
        name="rope",
    )(p, p, p, *tabs)


def _rope_tables(pos):
    half = ROPE_DIM // 2
    inv = ROPE_THETA ** (-jnp.arange(half, dtype=f32) / half)
    ang = pos.astype(f32)[:, None] * inv[None, :]
    cos, sin = jnp.cos(ang), jnp.sin(ang)
    n = pos.shape[0]
    rest = DIFF_DQK - ROPE_DIM
    one, zero, zh = jnp.ones((n, rest), f32), jnp.zeros((n, rest), f32), jnp.zeros((n, half), f32)
    c = jnp.concatenate([cos, cos, one], axis=1)
    s1 = jnp.concatenate([zh, sin, zero], axis=1)
    s2 = jnp.concatenate([-sin, zh, zero], axis=1)
    rep = LANES // DIFF_DQK
    return tuple(jnp.tile(t, (1, rep)) for t in (c, s1, s2))


INV_BASE = 16


def _block_masks(row, colm):
    masks = [(row // INV_BASE) == (colm // INV_BASE)]
    nb = INV_BASE
    while nb < row.shape[0]:
        masks.append(((row // nb) == (colm // nb) + 1) & ((colm // nb) % 2 == 0))
        nb *= 2
    return masks


def _tri_inverse(a_list, eye, masks):
    n = a_list[0].shape[0]
    ad = [jnp.where(masks[0], a, 0.0) for a in a_list]
    t = [eye - x for x in ad]
    p = [_dot(x, x) for x in ad]
    steps = int(math.log2(INV_BASE)) - 1
    for s in range(steps):
        if s + 1 < steps:
            tp = [_dot(jnp.concatenate([ti, pi], axis=0), pi) for ti, pi in zip(t, p)]
            t = [ti + x[:n] for ti, x in zip(t, tp)]
            p = [x[n:] for x in tp]
        else:
            t = [ti + _dot(ti, pi) for ti, pi in zip(t, p)]
    for m in masks[1:]:
        w = [_dot(jnp.where(m, a, 0.0), ti) for a, ti in zip(a_list, t)]
        t = [ti - _dot(ti, wi) for ti, wi in zip(t, w)]
    return t


def _gdn_body(qkv_ref, tail_ref, cw_ref, alog_ref, dtb_ref, s0_ref, c0_ref, o_ref, so_ref,
              xp_ref, s_ref, *, n_valid):
    c = pl.program_id(1)
    L = CHUNK

    @pl.when(c == 0)
    def _():
        xp_ref[...] = c0_ref[...]
        s_ref[...] = s0_ref[...]

    row = lax.broadcasted_iota(jnp.int32, (L, L), 0)
    colm = lax.broadcasted_iota(jnp.int32, (L, L), 1)
    incl = row >= colm
    strict = row > colm
    eye = jnp.where(row == colm, 1.0, 0.0).astype(f32)
    masks = _block_masks(row, colm)
    ltri = jnp.where(incl, 1.0, 0.0).astype(bf16)
    utri = jnp.where(row <= colm, 1.0, 0.0).astype(bf16)
    valid = (c * L + row) < n_valid
    cw = cw_ref[...]
    row8 = lax.broadcasted_iota(jnp.int32, (SUBLANES, CONV_CH), 0)
    l2 = lambda x: x * lax.rsqrt(jnp.sum(x * x, axis=-1, keepdims=True) + EPS)

    idx, q, k, v, beta_c, gam_c, dec, kdec_c, scale_c = [], [], [], [], [], [], [], [], []
    for sq in range(qkv_ref.shape[0]):
        u = qkv_ref[sq]
        prev = xp_ref[sq]
        y = u * cw[CONV_W - 1:CONV_W, :]
        for sh in range(1, CONV_W):
            us = pltpu.roll(u, sh, 0)
            top = jnp.where(row8 < sh, pltpu.roll(prev, sh, 0), us[0:SUBLANES, :])
            y = y + jnp.concatenate([top, us[SUBLANES:, :]], axis=0) * cw[CONV_W - 1 - sh:CONV_W - sh, :]
        xp_ref[sq] = u[L - SUBLANES:, :]
        y = y * jax.nn.sigmoid(y)

        t = tail_ref[sq]
        beta = jnp.where(valid, jax.nn.sigmoid(t), 0.0)
        z = t + dtb_ref[...]
        softplus = jnp.maximum(z, 0.0) + jnp.log(1.0 + jnp.exp(-jnp.abs(z)))
        g = jnp.where(valid, -jnp.exp(alog_ref[...]) * softplus, 0.0)
        g_col = _dot_exact_lhs(ltri, g)
        g_row = _dot_exact_rhs(g.T, utri)
        gam = jnp.exp(g_col)
        g_last = g_col[L - 1:L, :]
        kdec = jnp.exp(g_last - g_col)
        s_scale = jnp.exp(g_last)
        for h in range(GDN_HEADS):
            gl = GDN_HEADS + h
            idx.append((sq, h))
            q.append(l2(y[:, h * GDN_DK:(h + 1) * GDN_DK]) * (GDN_DK ** -0.5))
            k.append(l2(y[:, GDN_QK_W + h * GDN_DK:GDN_QK_W + (h + 1) * GDN_DK]))
            v.append(y[:, 2 * GDN_QK_W + h * GDN_DV:2 * GDN_QK_W + (h + 1) * GDN_DV])
            beta_c.append(beta[:, h:h + 1])
            gam_c.append(gam[:, gl:gl + 1])
            dec.append(jnp.exp(jnp.where(incl, g_col[:, gl:gl + 1] - g_row[gl:gl + 1, :], NEG)))
            kdec_c.append(kdec[:, gl:gl + 1])
            scale_c.append(s_scale[:, gl:gl + 1])

    ch = range(len(idx))
    kb = [k[i] * beta_c[i] for i in ch]
    k_bf = [x.astype(bf16) for x in k]
    a = [_dot_nt(kb[i], k_bf[i]) * jnp.where(strict, dec[i], 0.0) for i in ch]
    qk = [_dot_nt(q[i], k_bf[i]) * dec[i] for i in ch]
    tinv = _tri_inverse(a, eye, masks)
    sol = [_dot(tinv[i], jnp.concatenate([kb[i] * gam_c[i], v[i] * beta_c[i]], axis=1)) for i in ch]
    s_old = [s_ref[sq, h] for sq, h in idx]
    u = [sol[i][:, GDN_DK:] - _dot(sol[i][:, :GDN_DK], s_old[i]) for i in ch]
    o = [_dot(q[i] * gam_c[i], s_old[i]) + _dot(qk[i], u[i]) for i in ch]
    for i, (sq, h) in enumerate(idx):
        o_ref[sq, :, h * GDN_DV:(h + 1) * GDN_DV] = o[i]
    for i, (sq, h) in enumerate(idx):
        kd = (k[i] * kdec_c[i]).T
        s_ref[sq, h] = scale_c[i] * s_old[i] + _dot(kd, u[i])

    @pl.when(c == pl.num_programs(1) - 1)
    def _():
        so_ref[...] = s_ref[...]


GDN_SEQS_PER_STEP = 2


def _gdn(qkv_src, tail, conv_w, alog_row, dtb_row, s0, c0, layer, n_valid):
    n_seq, rows, _ = qkv_src.shape
    g = GDN_SEQS_PER_STEP
    assert n_seq % g == 0 and rows % CHUNK == 0
    state = pl.BlockSpec((g, GDN_HEADS, GDN_DK, GDN_DV), lambda s, c: (s, 0, 0, 0))
    return pl.pallas_call(
        functools.partial(_gdn_body, n_valid=n_valid),
        out_shape=(jax.ShapeDtypeStruct((n_seq, rows, GDN_V_W), f32),
                   jax.ShapeDtypeStruct((n_seq, GDN_HEADS, GDN_DK, GDN_DV), f32)),
        grid=(n_seq // g, rows // CHUNK),
        in_specs=[
            pl.BlockSpec((g, CHUNK, CONV_CH), lambda s, c: (s, c, 0)),
            pl.BlockSpec((g, CHUNK, LANES), lambda s, c: (s, c, 0)),
            pl.BlockSpec((None, CONV_W, CONV_CH), lambda s, c: (layer, 0, 0)),
            pl.BlockSpec((None, 1, LANES), lambda s, c: (layer, 0, 0)),
            pl.BlockSpec((None, 1, LANES), lambda s, c: (layer, 0, 0)),
            state,
            pl.BlockSpec((g, SUBLANES, CONV_CH), lambda s, c: (s, 0, 0)),
        ],
        out_specs=(pl.BlockSpec((g, CHUNK, GDN_V_W), lambda s, c: (s, c, 0)), state),
        scratch_shapes=[pltpu.VMEM((g, SUBLANES, CONV_CH), f32),
                        pltpu.VMEM((g, GDN_HEADS, GDN_DK, GDN_DV), f32)],
        compiler_params=_params("parallel", "arbitrary"),
        name="gdn",
    )(qkv_src, tail, conv_w, alog_row, dtb_row, s0, c0)


def _lambda(lv_ref, lam_init):
    lv = lv_ref[...]
    a = jnp.sum(lv[0:1, :] * lv[1:2, :], axis=-1, keepdims=True)
    b = jnp.sum(lv[2:3, :] * lv[3:4, :], axis=-1, keepdims=True)
    return jnp.exp(a) - jnp.exp(b) + lam_init


ATTN_ROW_TILE = 64
ATTN_HEADS_PER_STEP = 4


def _attn_body(lv_ref, q_ref, k_ref, v_ref, o_ref, q2_ref, s_ref, p_ref, m_ref, l_ref, al_ref, acc_ref,
               *, tq, lam_init):
    qi = pl.program_id(2)
    rt = ATTN_ROW_TILE
    hp = ATTN_HEADS_PER_STEP
    units = range(2 * hp)
    head_lanes = [slice(h * LANES, (h + 1) * LANES) for h in range(hp)]
    lane = lax.broadcasted_iota(jnp.int32, (tq, LANES), 1)
    for h in range(hp):
        q = q_ref[:, head_lanes[h]].astype(f32)
        q2_ref[2 * h] = jnp.where(lane < DIFF_DQK, q, 0.0).astype(bf16)
        q2_ref[2 * h + 1] = jnp.where(lane >= DIFF_DQK, q, 0.0).astype(bf16)
    m_ref[...] = jnp.full_like(m_ref, NEG)
    l_ref[...] = jnp.zeros_like(l_ref)
    acc_ref[...] = jnp.zeros_like(acc_ref)

    def scores(kj, slot):
        start = pl.multiple_of(kj * tq, tq)
        for u in units:
            kb = k_ref[pl.ds(start, tq), head_lanes[u // 2]]
            s_ref[slot, u] = lax.dot_general(q2_ref[u], kb, (((1,), (1,)), ((), ())),
                                             preferred_element_type=f32)

    def softmax_pv(kj, slot, masked):
        start = pl.multiple_of(kj * tq, tq)
        tiles = [slice(r * rt, (r + 1) * rt) for r in range(tq // rt)]
        lanes = [slice(j * LANES, (j + 1) * LANES) for j in range(tq // LANES)]
        wide = lambda x: jnp.broadcast_to(x, (rt, LANES))
        for u in units:
            for rows in tiles:
                s = s_ref[slot, u, rows, :]
                if masked:
                    ri = rows.start + lax.broadcasted_iota(jnp.int32, (rt, tq), 0)
                    ci = lax.broadcasted_iota(jnp.int32, (rt, tq), 1)
                    s = jnp.where(ri >= ci, s, NEG)
                    s_ref[slot, u, rows, :] = s
                m_old = m_ref[u, rows, :]
                m_new = jnp.maximum(m_old, wide(jnp.max(s, axis=-1, keepdims=True)))
                m_ref[u, rows, :] = m_new
                al_ref[u, rows, :] = jnp.exp(m_old - m_new)
        for u in units:
            for rows in tiles:
                m_t = m_ref[u, rows, :]
                ps = [jnp.exp(s_ref[slot, u, rows, ln] - m_t) for ln in lanes]
                for ln, pj in zip(lanes, ps):
                    p_ref[u, rows, ln] = pj.astype(bf16)
                psum = wide(jnp.sum(functools.reduce(jnp.add, ps), axis=-1, keepdims=True))
                l_ref[u, rows, :] = al_ref[u, rows, :] * l_ref[u, rows, :] + psum
            vb = v_ref[pl.ds(start, tq), head_lanes[u // 2]]
            acc_ref[u] = acc_ref[u] * al_ref[u] + jnp.dot(p_ref[u], vb, preferred_element_type=f32)

    scores(0, 0)

    def block(kj, slot):
        scores(kj + 1, 1 - slot)
        softmax_pv(kj, slot, False)

    def pair(j2, carry):
        block(2 * j2, 0)
        block(2 * j2 + 1, 1)
        return carry

    lax.fori_loop(0, qi // 2, pair, 0)
    odd = qi % 2 == 1

    @pl.when(odd)
    def _():
        block(qi - 1, 0)
        softmax_pv(qi, 1, True)

    @pl.when(jnp.logical_not(odd))
    def _():
        softmax_pv(qi, 0, True)

    lam = _lambda(lv_ref, lam_init)
    for h in range(hp):
        o_ref[:, head_lanes[h]] = (acc_ref[2 * h] / l_ref[2 * h]
                                   - lam * (acc_ref[2 * h + 1] / l_ref[2 * h + 1]))


def _attn(lam_vecs, q, k, v, layer, lam_init, n_batch, rows_per_seq):
    tq = _pick_tile(rows_per_seq, 384, CHUNK)
    nq = rows_per_seq // tq
    hp = ATTN_HEADS_PER_STEP
    nu = 2 * hp
    assert tq % ATTN_ROW_TILE == 0 and DIFF_HEADS % hp == 0
    return pl.pallas_call(
        functools.partial(_attn_body, tq=tq, lam_init=lam_init),
        out_shape=jax.ShapeDtypeStruct((n_batch * rows_per_seq, DIFF_HEADS * DIFF_DV), f32),
        grid=(n_batch, DIFF_HEADS // hp, nq),
        in_specs=[
            pl.BlockSpec((None, 4, DIFF_DQK), lambda b, h, i: (layer, 0, 0)),
            pl.BlockSpec((tq, hp * LANES), lambda b, h, i: (b * nq + i, h)),
            pl.BlockSpec((rows_per_seq, hp * LANES), lambda b, h, i: (b, h)),
            pl.BlockSpec((rows_per_seq, hp * LANES), lambda b, h, i: (b, h)),
        ],
        out_specs=pl.BlockSpec((tq, hp * DIFF_DV), lambda b, h, i: (b * nq + i, h)),
        scratch_shapes=[pltpu.VMEM((nu, tq, LANES), bf16),
                        pltpu.VMEM((2, nu, tq, tq), f32),
                        pltpu.VMEM((nu, tq, tq), bf16),
                        pltpu.VMEM((nu, tq, LANES), f32), pltpu.VMEM((nu, tq, LANES), f32),
                        pltpu.VMEM((nu, tq, LANES), f32),
                        pltpu.VMEM((nu, tq, DIFF_DV), f32)],
        compiler_params=_params("parallel", "parallel", "arbitrary"),
        name="diff_attn",
    )(lam_vecs, q, k, v)


def _decode_body(pt_ref, lv_ref, q_ref, *refs, n_pages_step, lam_init, n_new):
    del pt_ref
    k_refs = refs[:n_pages_step]
    v_refs = refs[n_pages_step:2 * n_pages_step]
    kn_ref, vn_ref, o_ref, m_ref, l_ref, acc_ref = refs[2 * n_pages_step:]
    step = pl.program_id(1)
    nq = q_ref.shape[0]

    @pl.when(step == 0)
    def _():
        m_ref[...] = jnp.full_like(m_ref, NEG)
        l_ref[...] = jnp.zeros_like(l_ref)
        acc_ref[...] = jnp.zeros_like(acc_ref)

    qt = q_ref[...]

    def process(ks, vs, mask):
        ss = [jnp.dot(qt, kr[...].astype(bf16), preferred_element_type=f32) for kr in ks]
        if mask is not None:
            ss = [jnp.where(mask, s, NEG) for s in ss]
        mx = functools.reduce(jnp.maximum, [jnp.max(s, axis=1, keepdims=True) for s in ss])
        m_old = m_ref[...]
        m_new = jnp.maximum(m_old, mx)
        alpha = jnp.exp(m_old - m_new)
        ps = [jnp.exp(s - m_new) for s in ss]
        l_ref[...] = alpha * l_ref[...] + functools.reduce(
            jnp.add, [jnp.sum(p, axis=1, keepdims=True) for p in ps])
        m_ref[...] = m_new
        for h in range(DIFF_HEADS):
            rows = slice(h * SUBLANES, (h + 1) * SUBLANES)
            acc = acc_ref[h] * alpha[rows]
            for p, vr in zip(ps, vs):
                vh = vr[pl.ds(h, PAGE_SIZE, stride=DIFF_HEADS), :].astype(bf16)
                acc = acc + jnp.dot(p[rows].astype(bf16), vh, preferred_element_type=f32)
            acc_ref[h] = acc

    process(k_refs, v_refs, None)

    @pl.when(step == pl.num_programs(1) - 1)
    def _():
        tok = lax.broadcasted_iota(jnp.int32, (nq, PAGE_SIZE), 0) % n_new
        key = lax.broadcasted_iota(jnp.int32, (nq, PAGE_SIZE), 1)
        process([kn_ref], [vn_ref], key <= tok)
        lam = _lambda(lv_ref, lam_init)
        linv = 1.0 / l_ref[...]
        for h in range(DIFF_HEADS):
            a = acc_ref[h] * linv[h * SUBLANES:(h + 1) * SUBLANES]
            o_ref[:, h * DIFF_DV:(h + 1) * DIFF_DV] = a - lam * pltpu.roll(a, n_new, 0)


def _decode(page_table, lam_vecs, qt, cache_kt, cache_v, kt_new, v_new, layer, lam_init, n_new):
    n_seq, n_pages = page_table.shape
    g = PAGES_PER_STEP
    nq = qt.shape[1]
    assert n_pages % g == 0 and 2 * n_new == SUBLANES and nq == DIFF_HEADS * SUBLANES
    kr, vr = cache_kt.shape[2], cache_v.shape[2]

    def page_spec(rows, u):
        return pl.BlockSpec((None, None, rows, LANES),
                            lambda b, s, pt: (layer, pt[b * n_pages + s * g + u], 0, 0))

    new_spec = lambda rows: pl.BlockSpec((None, rows, LANES), lambda b, s, pt: (b, 0, 0))
    grid_spec = pltpu.PrefetchScalarGridSpec(
        num_scalar_prefetch=1,
        grid=(n_seq, n_pages // g),
        in_specs=[pl.BlockSpec((None, 4, DIFF_DQK), lambda b, s, pt: (layer, 0, 0)),
                  pl.BlockSpec((None, nq, kr), lambda b, s, pt: (b, 0, 0))]
                 + [page_spec(kr, u) for u in range(g)] + [page_spec(vr, u) for u in range(g)]
                 + [new_spec(kr), new_spec(vr)],
        out_specs=pl.BlockSpec((None, SUBLANES, DIFF_HEADS * DIFF_DV), lambda b, s, pt: (b, 0, 0)),
        scratch_shapes=[pltpu.VMEM((nq, 1), f32), pltpu.VMEM((nq, 1), f32),
                        pltpu.VMEM((DIFF_HEADS, SUBLANES, DIFF_DV), f32)],
    )
    return pl.pallas_call(
        functools.partial(_decode_body, n_pages_step=g, lam_init=lam_init, n_new=n_new),
        out_shape=jax.ShapeDtypeStruct((n_seq, SUBLANES, DIFF_HEADS * DIFF_DV), f32),
        grid_spec=grid_spec,
        compiler_params=_params("parallel", "arbitrary"),
        name="decode_attn",
    )(page_table.reshape(-1), lam_vecs, qt, *([cache_kt] * g), *([cache_v] * g), kt_new, v_new)


def _merge_body(x_ref, og_ref, z_ref, od_ref, ga_ref, gb_ref, gn_ref, sn_ref, wbg_ref, wbd_ref,
                wo_ref, o_ref, yg_ref, yd_ref, *, d_scale):
    gn, sn = gn_ref[...], sn_ref[...]
    for h in range(GDN_HEADS):
        sl = slice(h * GDN_DV, (h + 1) * GDN_DV)
        zz = z_ref[:, sl]
        yg_ref[:, sl] = (_rms(og_ref[:, sl], gn) * (zz * jax.nn.sigmoid(zz))).astype(bf16)
    for h in range(DIFF_HEADS):
        sl = slice(h * DIFF_DV, (h + 1) * DIFF_DV)
        yd_ref[:, sl] = (_rms(od_ref[:, sl], sn) * d_scale).astype(bf16)
    mix = (jax.nn.sigmoid(ga_ref[...]) * jnp.dot(yg_ref[...], wbg_ref[...], preferred_element_type=f32)
           + jax.nn.sigmoid(gb_ref[...]) * jnp.dot(yd_ref[...], wbd_ref[...], preferred_element_type=f32))
    o_ref[...] = x_ref[...] + jnp.dot(mix.astype(bf16), wo_ref[...], preferred_element_type=f32)


def _merge(x, o_g, p, o_d, gn, sn, wbg, wbd, wo, layer, lam_init):
    m, d = x.shape
    tm = _pick_tile(m, 256)
    row = lambda i: (i, 0)
    wspec = lambda r, c: pl.BlockSpec((None, r, c), lambda i: (layer, 0, 0), pipeline_mode=pl.Buffered(1))
    return pl.pallas_call(
        functools.partial(_merge_body, d_scale=1.0 - lam_init),
        out_shape=jax.ShapeDtypeStruct((m, d), f32),
        grid=(m // tm,),
        in_specs=[
            pl.BlockSpec((tm, d), row),
            pl.BlockSpec((tm, GDN_V_W), row),
            pl.BlockSpec((tm, GDN_V_W), lambda i: (i, COL_Z)),
            pl.BlockSpec((tm, DIFF_W), row),
            pl.BlockSpec((tm, d), lambda i: (i, COL_GA * DIFF_W // d)),
            pl.BlockSpec((tm, d), lambda i: (i, COL_GB * DIFF_W // d)),
            pl.BlockSpec((None, 1, GDN_DV), lambda i: (layer, 0, 0)),
            pl.BlockSpec((None, 1, DIFF_DV), lambda i: (layer, 0, 0)),
            wspec(GDN_V_W, d), wspec(DIFF_W, d), wspec(d, d),
        ],
        out_specs=pl.BlockSpec((tm, d), row),
        scratch_shapes=[pltpu.VMEM((tm, GDN_V_W), bf16), pltpu.VMEM((tm, DIFF_W), bf16)],
        compiler_params=_params("parallel"),
        name="merge",
    )(x, o_g, p, o_d, p, p, gn, sn, wbg, wbd, wo)


def _final_body(x_ref, w_ref, o_ref):
    o_ref[...] = _rms(x_ref[...], w_ref[...])


def _final_norm(x, w):
    m, d = x.shape
    tm = _pick_tile(m, 768)
    return pl.pallas_call(
        _final_body,
        out_shape=jax.ShapeDtypeStruct((m, d), f32),
        grid=(m // tm,),
        in_specs=[pl.BlockSpec((tm, d), lambda i: (i, 0)), pl.BlockSpec((1, d), lambda i: (0, 0))],
        out_specs=pl.BlockSpec((tm, d), lambda i: (i, 0)),
        compiler_params=_params("parallel"),
        name="final_norm",
    )(x, w)


def kernel(x_prompt, x_sample, cache_k, cache_v, state_delta, state_conv, page_table, meta_tokens,
           ffn1_norm, ffn1_gate, ffn1_up, ffn1_down, mix_norm, w_in, conv_w, gdn_a_log, gdn_dt_bias,
           gdn_out_norm, diff_lambda, diff_subln, w_branch_gdn, w_branch_diff, w_out, ffn2_norm,
           ffn2_gate, ffn2_up, ffn2_down, final_norm):
    nb, seq, d = x_prompt.shape
    ns, ls, _ = x_sample.shape
    depth = w_in.shape[0]
    n_tok = seq + N_META
    rps = -(-n_tok // CHUNK) * CHUNK
    n_smp = ns * ls
    assert rps - n_tok >= n_smp and d == 2 * DIFF_W
    past = page_table.shape[1] * PAGE_SIZE
    n_pool = cache_k.shape[1]

    x = jnp.concatenate([jnp.broadcast_to(meta_tokens[None], (nb, N_META, d)), x_prompt,
                         jnp.zeros((nb, rps - n_tok, d), f32)], axis=1).reshape(nb * rps, d)
    s_lo, s_hi = n_tok, n_tok + n_smp
    x = x.at[s_lo:s_hi].set(x_sample.reshape(n_smp, d))

    pos = jnp.tile(jnp.arange(rps, dtype=jnp.int32), nb)
    pos = pos.at[s_lo:s_hi].set(jnp.tile(past + jnp.arange(ls, dtype=jnp.int32), ns))
    tabs = _rope_tables(pos)

    gate_lo = CONV_CH + GDN_V_W
    gate_hi = gate_lo + 2 * GDN_HEADS
    diff_hi = gate_hi + 3 * DIFF_W
    w_main = jnp.swapaxes(w_in, 1, 2).astype(bf16)
    w_tail = jnp.pad(w_main[:, gate_lo:gate_hi], ((0, 0), (0, LANES - 2 * GDN_HEADS), (0, 0)))
    cast = lambda w: w.astype(bf16)
    f1g, f1u, f1d, f2g, f2u, f2d = map(cast, (ffn1_gate, ffn1_up, ffn1_down, ffn2_gate, ffn2_up, ffn2_down))
    wbg, wbd, wo = map(cast, (w_branch_gdn, w_branch_diff, w_out))
    row3 = lambda w: w.reshape(depth, 1, -1)
    n1, n2, nm, gn, sn = map(row3, (ffn1_norm, ffn2_norm, mix_norm, gdn_out_norm, diff_subln))
    lane_pad = lambda w: jnp.pad(w, ((0, 0), (GDN_HEADS, LANES - 2 * GDN_HEADS))).reshape(depth, 1, LANES)
    alog_row, dtb_row = lane_pad(gdn_a_log), lane_pad(gdn_dt_bias)

    ck = jnp.transpose(cache_k, (0, 1, 3, 4, 5, 2)).reshape(depth, n_pool, DIFF_W, PAGE_SIZE)
    cv = cache_v.reshape(depth, n_pool, PAGE_SIZE * DIFF_HEADS, DIFF_DV)
    zero_state = jnp.zeros((nb, GDN_HEADS, GDN_DK, GDN_DV), f32)
    zero_conv = jnp.zeros((nb, SUBLANES, CONV_CH), f32)
    conv_pad = jnp.pad(state_conv, ((0, 0), (0, 0), (SUBLANES - (CONV_W - 1), 0), (0, 0)))

    def smp_chunks(a):
        w = a.shape[-1]
        return jnp.pad(a.reshape(ns, ls, w), ((0, 0), (0, CHUNK - ls), (0, 0))).reshape(ns * CHUNK, w)

    outs = [[] for _ in range(8)]
    for l in range(depth):
        lam_init = 0.8 - 0.6 * math.exp(-0.3 * l)
        x = _ffn(x, n1, f1g, f1u, f1d, l)
        p, tail = _inproj(x, nm, w_main, w_tail, l)
        q_bf, k_bf, v_bf, k_f = _rope(p, tabs)

        o_g, st_p = _gdn(p.reshape(nb, rps, -1), tail.reshape(nb, rps, LANES), conv_w, alog_row, dtb_row,
                         zero_state, zero_conv, l, n_tok)
        qkv_s = p[s_lo:s_hi, :CONV_CH]
        og_s, st_s = _gdn(smp_chunks(qkv_s).reshape(ns, CHUNK, CONV_CH),
                          smp_chunks(tail[s_lo:s_hi]).reshape(ns, CHUNK, LANES), conv_w, alog_row, dtb_row,
                          state_delta[l], conv_pad[l], l, ls)
        o_g = o_g.reshape(nb * rps, GDN_V_W).at[s_lo:s_hi].set(og_s[:, :ls].reshape(n_smp, GDN_V_W))

        o_d = _attn(diff_lambda, q_bf, k_bf, v_bf, l, lam_init, nb, rps)
        q_s = q_bf[s_lo:s_hi].reshape(ns, ls, DIFF_HEADS, 2, DIFF_DQK)
        hc = DIFF_HEADS * 2
        eye_hc = jnp.eye(hc, dtype=bf16).reshape(DIFF_HEADS, 2, 1, DIFF_HEADS, 2, 1)
        qt = (jnp.transpose(q_s, (0, 2, 3, 1, 4))[:, :, :, :, None, None, :]
              * eye_hc[None, :, :, :, :, :, :]).reshape(ns, hc * ls, DIFF_W)
        kt_new = jnp.pad(jnp.transpose(k_f[s_lo:s_hi].reshape(ns, ls, DIFF_W), (0, 2, 1)),
                         ((0, 0), (0, 0), (0, PAGE_SIZE - ls)))
        v_s = p[s_lo:s_hi, COL_DV * DIFF_W:(COL_DV + 1) * DIFF_W]
        v_new = jnp.pad(v_s.reshape(ns, ls, DIFF_HEADS, DIFF_DV),
                        ((0, 0), (0, PAGE_SIZE - ls), (0, 0), (0, 0))).reshape(ns, PAGE_SIZE * DIFF_HEADS, DIFF_DV)
        od_s = _decode(page_table, diff_lambda, qt, ck, cv, kt_new, v_new, l, lam_init, ls)
        o_d = o_d.at[s_lo:s_hi].set(od_s[:, :ls].reshape(n_smp, DIFF_W))

        x = _merge(x, o_g, p, o_d, gn, sn, wbg, wbd, wo, l, lam_init)
        x = _ffn(x, n2, f2g, f2u, f2d, l)

        seq_view = lambda a: a.reshape(nb, rps, a.shape[-1])
        outs[0].append(seq_view(k_f)[:, :n_tok].reshape(nb, n_tok, DIFF_HEADS, 2, DIFF_DQK))
        outs[1].append(seq_view(p[:, COL_DV * DIFF_W:(COL_DV + 1) * DIFF_W])[:, :n_tok]
                       .reshape(nb, n_tok, DIFF_HEADS, DIFF_DV))
        outs[2].append(st_p)
        outs[3].append(seq_view(p)[:, n_tok - (CONV_W - 1):n_tok, :CONV_CH])
        outs[4].append(k_f[s_lo:s_hi].reshape(ns, ls, DIFF_HEADS, 2, DIFF_DQK))
        outs[5].append(v_s.reshape(ns, ls, DIFF_HEADS, DIFF_DV))
        outs[6].append(st_s)
        outs[7].append(jnp.concatenate([state_conv[l], qkv_s.reshape(ns, ls, CONV_CH)], axis=1)[:, -(CONV_W - 1):])

    y = _final_norm(x, final_norm.reshape(1, d))
    y_prompt = y.reshape(nb, rps, d)[:, N_META:n_tok]
    y_sample = y[s_lo:s_hi].reshape(ns, ls, d)
    return (y_prompt, y_sample) + tuple(jnp.stack(o) for o in outs)
```

```python
import functools
import math

import jax
import jax.numpy as jnp
from jax import lax
from jax.experimental import pallas as pl
from jax.experimental.pallas import tpu as pltpu

N_META = 16
GDN_HEADS = 8
GDN_DK = 128
GDN_DV = 128
CONV_W = 4
DIFF_HEADS = 8
DIFF_DQK = 64
DIFF_DV = 128
ROPE_THETA = 500000.0
ROPE_DIM = DIFF_DQK // 4
PAGE_SIZE = 128
EPS = 1e-6

CHUNK = 128
LANES = 128
SUBLANES = 8
NEG = -1e30
VMEM_LIMIT = 56 * 1024 * 1024
PAGES_PER_STEP = 16

GDN_QK_W = GDN_HEADS * GDN_DK
GDN_V_W = GDN_HEADS * GDN_DV
CONV_CH = 2 * GDN_QK_W + GDN_V_W
DIFF_W = DIFF_HEADS * 2 * DIFF_DQK
COL_Z, COL_GA, COL_GB, COL_DQ, COL_DK, COL_DV = 3, 4, 6, 8, 9, 10

f32 = jnp.float32
bf16 = jnp.bfloat16


def _params(*sem):
    return pltpu.CompilerParams(dimension_semantics=sem, vmem_limit_bytes=VMEM_LIMIT)


def _pick_tile(n, target, mult=SUBLANES):
    best = None
    for t in range(mult, min(n, target) + 1, mult):
        if n % t == 0:
            best = t
    assert best is not None, (n, target, mult)
    return best


def _rms(x, w):
    return x * lax.rsqrt(jnp.mean(x * x, axis=-1, keepdims=True) + EPS) * w


def _dot(a, b):
    return jnp.dot(a.astype(bf16), b.astype(bf16), preferred_element_type=f32)


def _dot_nt(a, b):
    return lax.dot_general(a.astype(bf16), b.astype(bf16), (((1,), (1,)), ((), ())),
                           preferred_element_type=f32)


def _dot_exact_lhs(a_bf, x):
    x1 = x.astype(bf16)
    r1 = x - x1.astype(f32)
    x2 = r1.astype(bf16)
    x3 = (r1 - x2.astype(f32)).astype(bf16)
    d = functools.partial(jnp.dot, preferred_element_type=f32)
    return d(a_bf, x1) + (d(a_bf, x2) + d(a_bf, x3))


def _dot_exact_rhs(x, b_bf):
    x1 = x.astype(bf16)
    r1 = x - x1.astype(f32)
    x2 = r1.astype(bf16)
    x3 = (r1 - x2.astype(f32)).astype(bf16)
    d = functools.partial(jnp.dot, preferred_element_type=f32)
    return d(x1, b_bf) + (d(x2, b_bf) + d(x3, b_bf))


def _ffn_body(x_ref, nw_ref, wg_ref, wu_ref, wd_ref, o_ref, h_ref):
    j = pl.program_id(1)

    @pl.when(j == 0)
    def _():
        h_ref[...] = _rms(x_ref[...], nw_ref[...]).astype(bf16)
        o_ref[...] = jnp.zeros_like(o_ref)

    h = h_ref[...]
    g = jnp.dot(h, wg_ref[...], preferred_element_type=f32)
    u = jnp.dot(h, wu_ref[...], preferred_element_type=f32)
    a = (g * jax.nn.sigmoid(g) * u).astype(bf16)
    o_ref[...] += jnp.dot(a, wd_ref[...], preferred_element_type=f32)

    @pl.when(j == pl.num_programs(1) - 1)
    def _():
        o_ref[...] = x_ref[...] + 0.5 * o_ref[...]


def _ffn(x, nw, wg, wu, wd, layer):
    m, d = x.shape
    ff = wg.shape[-1]
    tm = _pick_tile(m, 768)
    tf = _pick_tile(ff, 512, LANES)
    return pl.pallas_call(
        _ffn_body,
        out_shape=jax.ShapeDtypeStruct((m, d), f32),
        grid=(m // tm, ff // tf),
        in_specs=[
            pl.BlockSpec((tm, d), lambda i, j: (i, 0)),
            pl.BlockSpec((None, 1, d), lambda i, j: (layer, 0, 0)),
            pl.BlockSpec((None, d, tf), lambda i, j: (layer, 0, j)),
            pl.BlockSpec((None, d, tf), lambda i, j: (layer, 0, j)),
            pl.BlockSpec((None, tf, d), lambda i, j: (layer, j, 0)),
        ],
        out_specs=pl.BlockSpec((tm, d), lambda i, j: (i, 0)),
        scratch_shapes=[pltpu.VMEM((tm, d), bf16)],
        compiler_params=_params("parallel", "arbitrary"),
        name="ffn",
    )(x, nw, wg, wu, wd)


def _inproj_body(x_ref, nw_ref, w_ref, wt_ref, o_ref, t_ref, h_ref):
    @pl.when(pl.program_id(1) == 0)
    def _():
        h = _rms(x_ref[...], nw_ref[...]).astype(bf16)
        h_ref[...] = h
        t_ref[...] = _dot_nt(h, wt_ref[...])

    o_ref[...] = _dot_nt(h_ref[...], w_ref[0])


def _inproj(x, nw, w_t, w_tail, layer):
    m, d = x.shape
    tn = DIFF_W
    gate_lo = CONV_CH + GDN_V_W
    gate_hi = gate_lo + 2 * GDN_HEADS
    n = w_t.shape[1] - 2 * GDN_HEADS
    tm = _pick_tile(m, 768)

    def row_start(j):
        moved = jnp.where(j < COL_DQ, j + (COL_DV + 1 - COL_DQ), j - (COL_DQ - COL_GA))
        return pl.multiple_of(jnp.where(j < COL_GA, j * tn, gate_hi + (moved - COL_GA) * tn), 2 * SUBLANES)

    return pl.pallas_call(
        _inproj_body,
        out_shape=(jax.ShapeDtypeStruct((m, n), f32), jax.ShapeDtypeStruct((m, LANES), f32)),
        grid=(m // tm, n // tn),
        in_specs=[
            pl.BlockSpec((tm, d), lambda i, j: (i, 0)),
            pl.BlockSpec((None, 1, d), lambda i, j: (layer, 0, 0)),
            pl.BlockSpec((pl.Element(1), pl.Element(tn), pl.Element(d)), lambda i, j: (layer, row_start(j), 0)),
            pl.BlockSpec((None, LANES, d), lambda i, j: (layer, 0, 0)),
        ],
        out_specs=(pl.BlockSpec((tm, tn), lambda i, j: (i, j)),
                   pl.BlockSpec((tm, LANES), lambda i, j: (i, 0))),
        scratch_shapes=[pltpu.VMEM((tm, d), bf16)],
        compiler_params=_params("parallel", "arbitrary"),
        name="inproj",
    )(x, nw, w_t, w_tail)


def _rope_body(dq_ref, dk_ref, dv_ref, c_ref, s1_ref, s2_ref, q_ref, kb_ref, vb_ref, kf_ref):
    c, s1, s2 = c_ref[...], s1_ref[...], s2_ref[...]
    for h in range(DIFF_HEADS):
        sl = slice(h * LANES, (h + 1) * LANES)
        for src, scale, dsts in ((dq_ref, DIFF_DQK ** -0.5, (q_ref,)), (dk_ref, None, (kb_ref, kf_ref))):
            x = src[:, sl]
            half = ROPE_DIM // 2
            r = x * c + pltpu.roll(x, half, 1) * s1 + pltpu.roll(x, LANES - half, 1) * s2
            if scale is not None:
                r = r * scale
            for dst in dsts:
                dst[:, sl] = r.astype(dst.dtype)
    vb_ref[...] = dv_ref[...].astype(bf16)


def _rope(p, tabs):
    m = p.shape[0]
    tr = _pick_tile(m, 384, 16)
    row = lambda i: (i, 0)
    col = lambda cb: pl.BlockSpec((tr, DIFF_W), lambda i: (i, cb))
    tab = pl.BlockSpec((tr, LANES), row)
    return pl.pallas_call(
        _rope_body,
        out_shape=(jax.ShapeDtypeStruct((m, DIFF_W), bf16),) * 3 + (jax.ShapeDtypeStruct((m, DIFF_W), f32),),
        grid=(m // tr,),
        in_specs=[col(COL_DQ), col(COL_DK), col(COL_DV), tab, tab, tab],
        out_specs=(pl.BlockSpec((tr, DIFF_W), row),) * 4,
        compiler_params=_params("parallel"),
        name="rope",
    )(p, p, p, *tabs)


def _rope_tables(pos):
    half = ROPE_DIM // 2
    inv = ROPE_THETA ** (-jnp.arange(half, dtype=f32) / half)
    ang = pos.astype(f32)[:, None] * inv[None, :]
    cos, sin = jnp.cos(ang), jnp.sin(ang)
    n = pos.shape[0]
    rest = DIFF_DQK - ROPE_DIM
    one, zero, zh = jnp.ones((n, rest), f32), jnp.zeros((n, rest), f32), jnp.zeros((n, half), f32)
    c = jnp.concatenate([cos, cos, one], axis=1)
    s1 = jnp.concatenate([zh, sin, zero], axis=1)
    s2 = jnp.concatenate([-sin, zh, zero], axis=1)
    rep = LANES // DIFF_DQK
    return tuple(jnp.tile(t, (1, rep)) for t in (c, s1, s2))


INV_BASE = 16


def _block_masks(row, colm):
    masks = [(row // INV_BASE) == (colm // INV_BASE)]
    nb = INV_BASE
    while nb < row.shape[0]:
        masks.append(((row // nb) == (colm // nb) + 1) & ((colm // nb) % 2 == 0))
        nb *= 2
    return masks


def _tri_inverse(a_list, eye, masks):
    n = a_list[0].shape[0]
    ad = [jnp.where(masks[0], a, 0.0) for a in a_list]
    t = [eye - x for x in ad]
    p = [_dot(x, x) for x in ad]
    steps = int(math.log2(INV_BASE)) - 1
    for s in range(steps):
        if s + 1 < steps:
            tp = [_dot(jnp.concatenate([ti, pi], axis=0), pi) for ti, pi in zip(t, p)]
            t = [ti + x[:n] for ti, x in zip(t, tp)]
            p = [x[n:] for x in tp]
        else:
            t = [ti + _dot(ti, pi) for ti, pi in zip(t, p)]
    for m in masks[1:]:
        w = [_dot(jnp.where(m, a, 0.0), ti) for a, ti in zip(a_list, t)]
        t = [ti - _dot(ti, wi) for ti, wi in zip(t, w)]
    return t


def _gdn_body(qkv_ref, tail_ref, cw_ref, alog_ref, dtb_ref, s0_ref, c0_ref, o_ref, so_ref,
              xp_ref, s_ref, *, n_valid):
    c = pl.program_id(1)
    L = CHUNK

    @pl.when(c == 0)
    def _():
        xp_ref[...] = c0_ref[...]
        s_ref[...] = s0_ref[...]

    row = lax.broadcasted_iota(jnp.int32, (L, L), 0)
    colm = lax.broadcasted_iota(jnp.int32, (L, L), 1)
    incl = row >= colm
    strict = row > colm
    eye = jnp.where(row == colm, 1.0, 0.0).astype(f32)
    masks = _block_masks(row, colm)
    ltri = jnp.where(incl, 1.0, 0.0).astype(bf16)
    utri = jnp.where(row <= colm, 1.0, 0.0).astype(bf16)
    valid = (c * L + row) < n_valid
    cw = cw_ref[...]
    row8 = lax.broadcasted_iota(jnp.int32, (SUBLANES, CONV_CH), 0)
    l2 = lambda x: x * lax.rsqrt(jnp.sum(x * x, axis=-1, keepdims=True) + EPS)

    idx, q, k, v, beta_c, gam_c, dec, kdec_c, scale_c = [], [], [], [], [], [], [], [], []
    for sq in range(qkv_ref.shape[0]):
        u = qkv_ref[sq]
        prev = xp_ref[sq]
        y = u * cw[CONV_W - 1:CONV_W, :]
        for sh in range(1, CONV_W):
            us = pltpu.roll(u, sh, 0)
            top = jnp.where(row8 < sh, pltpu.roll(prev, sh, 0), us[0:SUBLANES, :])
            y = y + jnp.concatenate([top, us[SUBLANES:, :]], axis=0) * cw[CONV_W - 1 - sh:CONV_W - sh, :]
        xp_ref[sq] = u[L - SUBLANES:, :]
        y = y * jax.nn.sigmoid(y)

        t = tail_ref[sq]
        beta = jnp.where(valid, jax.nn.sigmoid(t), 0.0)
        z = t + dtb_ref[...]
        softplus = jnp.maximum(z, 0.0) + jnp.log(1.0 + jnp.exp(-jnp.abs(z)))
        g = jnp.where(valid, -jnp.exp(alog_ref[...]) * softplus, 0.0)
        g_col = _dot_exact_lhs(ltri, g)
        g_row = _dot_exact_rhs(g.T, utri)
        gam = jnp.exp(g_col)
        g_last = g_col[L - 1:L, :]
        kdec = jnp.exp(g_last - g_col)
        s_scale = jnp.exp(g_last)
        for h in range(GDN_HEADS):
            gl = GDN_HEADS + h
            idx.append((sq, h))
            q.append(l2(y[:, h * GDN_DK:(h + 1) * GDN_DK]) * (GDN_DK ** -0.5))
            k.append(l2(y[:, GDN_QK_W + h * GDN_DK:GDN_QK_W + (h + 1) * GDN_DK]))
            v.append(y[:, 2 * GDN_QK_W + h * GDN_DV:2 * GDN_QK_W + (h + 1) * GDN_DV])
            beta_c.append(beta[:, h:h + 1])
            gam_c.append(gam[:, gl:gl + 1])
            dec.append(jnp.exp(jnp.where(incl, g_col[:, gl:gl + 1] - g_row[gl:gl + 1, :], NEG)))
            kdec_c.append(kdec[:, gl:gl + 1])
            scale_c.append(s_scale[:, gl:gl + 1])

    ch = range(len(idx))
    kb = [k[i] * beta_c[i] for i in ch]
    k_bf = [x.astype(bf16) for x in k]
    a = [_dot_nt(kb[i], k_bf[i]) * jnp.where(strict, dec[i], 0.0) for i in ch]
    qk = [_dot_nt(q[i], k_bf[i]) * dec[i] for i in ch]
    tinv = _tri_inverse(a, eye, masks)
    sol = [_dot(tinv[i], jnp.concatenate([kb[i] * gam_c[i], v[i] * beta_c[i]], axis=1)) for i in ch]
    s_old = [s_ref[sq, h] for sq, h in idx]
    u = [sol[i][:, GDN_DK:] - _dot(sol[i][:, :GDN_DK], s_old[i]) for i in ch]
    o = [_dot(q[i] * gam_c[i], s_old[i]) + _dot(qk[i], u[i]) for i in ch]
    for i, (sq, h) in enumerate(idx):
        o_ref[sq, :, h * GDN_DV:(h + 1) * GDN_DV] = o[i]
    for i, (sq, h) in enumerate(idx):
        kd = (k[i] * kdec_c[i]).T
        s_ref[sq, h] = scale_c[i] * s_old[i] + _dot(kd, u[i])

    @pl.when(c == pl.num_programs(1) - 1)
    def _():
        so_ref[...] = s_ref[...]


GDN_SEQS_PER_STEP = 2


def _gdn(qkv_src, tail, conv_w, alog_row, dtb_row, s0, c0, layer, n_valid):
    n_seq, rows, _ = qkv_src.shape
    g = GDN_SEQS_PER_STEP
    assert n_seq % g == 0 and rows % CHUNK == 0
    state = pl.BlockSpec((g, GDN_HEADS, GDN_DK, GDN_DV), lambda s, c: (s, 0, 0, 0))
    return pl.pallas_call(
        functools.partial(_gdn_body, n_valid=n_valid),
        out_shape=(jax.ShapeDtypeStruct((n_seq, rows, GDN_V_W), f32),
                   jax.ShapeDtypeStruct((n_seq, GDN_HEADS, GDN_DK, GDN_DV), f32)),
        grid=(n_seq // g, rows // CHUNK),
        in_specs=[
            pl.BlockSpec((g, CHUNK, CONV_CH), lambda s, c: (s, c, 0)),
            pl.BlockSpec((g, CHUNK, LANES), lambda s, c: (s, c, 0)),
            pl.BlockSpec((None, CONV_W, CONV_CH), lambda s, c: (layer, 0, 0)),
            pl.BlockSpec((None, 1, LANES), lambda s, c: (layer, 0, 0)),
            pl.BlockSpec((None, 1, LANES), lambda s, c: (layer, 0, 0)),
            state,
            pl.BlockSpec((g, SUBLANES, CONV_CH), lambda s, c: (s, 0, 0)),
        ],
        out_specs=(pl.BlockSpec((g, CHUNK, GDN_V_W), lambda s, c: (s, c, 0)), state),
        scratch_shapes=[pltpu.VMEM((g, SUBLANES, CONV_CH), f32),
                        pltpu.VMEM((g, GDN_HEADS, GDN_DK, GDN_DV), f32)],
        compiler_params=_params("parallel", "arbitrary"),
        name="gdn",
    )(qkv_src, tail, conv_w, alog_row, dtb_row, s0, c0)


def _lambda(lv_ref, lam_init):
    lv = lv_ref[...]
    a = jnp.sum(lv[0:1, :] * lv[1:2, :], axis=-1, keepdims=True)
    b = jnp.sum(lv[2:3, :] * lv[3:4, :], axis=-1, keepdims=True)
    return jnp.exp(a) - jnp.exp(b) + lam_init


ATTN_ROW_TILE = 64
ATTN_HEADS_PER_STEP = 4


def _attn_body(lv_ref, q_ref, k_ref, v_ref, o_ref, q2_ref, s_ref, p_ref, m_ref, l_ref, al_ref, acc_ref,
               *, tq, lam_init):
    qi = pl.program_id(2)
    rt = ATTN_ROW_TILE
    hp = ATTN_HEADS_PER_STEP
    units = range(2 * hp)
    head_lanes = [slice(h * LANES, (h + 1) * LANES) for h in range(hp)]
    lane = lax.broadcasted_iota(jnp.int32, (tq, LANES), 1)
    for h in range(hp):
        q = q_ref[:, head_lanes[h]].astype(f32)
        q2_ref[2 * h] = jnp.where(lane < DIFF_DQK, q, 0.0).astype(bf16)
        q2_ref[2 * h + 1] = jnp.where(lane >= DIFF_DQK, q, 0.0).astype(bf16)
    m_ref[...] = jnp.full_like(m_ref, NEG)
    l_ref[...] = jnp.zeros_like(l_ref)
    acc_ref[...] = jnp.zeros_like(acc_ref)

    def scores(kj, slot):
        start = pl.multiple_of(kj * tq, tq)
        for u in units:
            kb = k_ref[pl.ds(start, tq), head_lanes[u // 2]]
            s_ref[slot, u] = lax.dot_general(q2_ref[u], kb, (((1,), (1,)), ((), ())),
                                             preferred_element_type=f32)

    def softmax_pv(kj, slot, masked):
        start = pl.multiple_of(kj * tq, tq)
        tiles = [slice(r * rt, (r + 1) * rt) for r in range(tq // rt)]
        lanes = [slice(j * LANES, (j + 1) * LANES) for j in range(tq // LANES)]
        wide = lambda x: jnp.broadcast_to(x, (rt, LANES))
        for u in units:
            for rows in tiles:
                s = s_ref[slot, u, rows, :]
                if masked:
                    ri = rows.start + lax.broadcasted_iota(jnp.int32, (rt, tq), 0)
                    ci = lax.broadcasted_iota(jnp.int32, (rt, tq), 1)
                    s = jnp.where(ri >= ci, s, NEG)
                    s_ref[slot, u, rows, :] = s
                m_old = m_ref[u, rows, :]
                m_new = jnp.maximum(m_old, wide(jnp.max(s, axis=-1, keepdims=True)))
                m_ref[u, rows, :] = m_new
                al_ref[u, rows, :] = jnp.exp(m_old - m_new)
        for u in units:
            for rows in tiles:
                m_t = m_ref[u, rows, :]
                ps = [jnp.exp(s_ref[slot, u, rows, ln] - m_t) for ln in lanes]
                for ln, pj in zip(lanes, ps):
                    p_ref[u, rows, ln] = pj.astype(bf16)
                psum = wide(jnp.sum(functools.reduce(jnp.add, ps), axis=-1, keepdims=True))
                l_ref[u, rows, :] = al_ref[u, rows, :] * l_ref[u, rows, :] + psum
            vb = v_ref[pl.ds(start, tq), head_lanes[u // 2]]
            acc_ref[u] = acc_ref[u] * al_ref[u] + jnp.dot(p_ref[u], vb, preferred_element_type=f32)

    scores(0, 0)

    def block(kj, slot):
        scores(kj + 1, 1 - slot)
        softmax_pv(kj, slot, False)

    def pair(j2, carry):
        block(2 * j2, 0)
        block(2 * j2 + 1, 1)
        return carry

    lax.fori_loop(0, qi // 2, pair, 0)
    odd = qi % 2 == 1

    @pl.when(odd)
    def _():
        block(qi - 1, 0)
        softmax_pv(qi, 1, True)

    @pl.when(jnp.logical_not(odd))
    def _():
        softmax_pv(qi, 0, True)

    lam = _lambda(lv_ref, lam_init)
    for h in range(hp):
        o_ref[:, head_lanes[h]] = (acc_ref[2 * h] / l_ref[2 * h]
                                   - lam * (acc_ref[2 * h + 1] / l_ref[2 * h + 1]))


def _attn(lam_vecs, q, k, v, layer, lam_init, n_batch, rows_per_seq):
    tq = _pick_tile(rows_per_seq, 384, CHUNK)
    nq = rows_per_seq // tq
    hp = ATTN_HEADS_PER_STEP
    nu = 2 * hp
    assert tq % ATTN_ROW_TILE == 0 and DIFF_HEADS % hp == 0
    return pl.pallas_call(
        functools.partial(_attn_body, tq=tq, lam_init=lam_init),
        out_shape=jax.ShapeDtypeStruct((n_batch * rows_per_seq, DIFF_HEADS * DIFF_DV), f32),
        grid=(n_batch, DIFF_HEADS // hp, nq),
        in_specs=[
            pl.BlockSpec((None, 4, DIFF_DQK), lambda b, h, i: (layer, 0, 0)),
            pl.BlockSpec((tq, hp * LANES), lambda b, h, i: (b * nq + i, h)),
            pl.BlockSpec((rows_per_seq, hp * LANES), lambda b, h, i: (b, h)),
            pl.BlockSpec((rows_per_seq, hp * LANES), lambda b, h, i: (b, h)),
        ],
        out_specs=pl.BlockSpec((tq, hp * DIFF_DV), lambda b, h, i: (b * nq + i, h)),
        scratch_shapes=[pltpu.VMEM((nu, tq, LANES), bf16),
                        pltpu.VMEM((2, nu, tq, tq), f32),
                        pltpu.VMEM((nu, tq, tq), bf16),
                        pltpu.VMEM((nu, tq, LANES), f32), pltpu.VMEM((nu, tq, LANES), f32),
                        pltpu.VMEM((nu, tq, LANES), f32),
                        pltpu.VMEM((nu, tq, DIFF_DV), f32)],
        compiler_params=_params("parallel", "parallel", "arbitrary"),
        name="diff_attn",
    )(lam_vecs, q, k, v)


def _decode_body(pt_ref, lv_ref, q_ref, *refs, n_pages_step, lam_init, n_new):
    del pt_ref
    k_refs = refs[:n_pages_step]
    v_refs = refs[n_pages_step:2 * n_pages_step]
    kn_ref, vn_ref, o_ref, m_ref, l_ref, acc_ref = refs[2 * n_pages_step:]
    step = pl.program_id(1)
    nq = q_ref.shape[0]

    @pl.when(step == 0)
    def _():
        m_ref[...] = jnp.full_like(m_ref, NEG)
        l_ref[...] = jnp.zeros_like(l_ref)
        acc_ref[...] = jnp.zeros_like(acc_ref)

    qt = q_ref[...]

    def process(ks, vs, mask):
        ss = [jnp.dot(qt, kr[...].astype(bf16), preferred_element_type=f32) for kr in ks]
        if mask is not None:
            ss = [jnp.where(mask, s, NEG) for s in ss]
        mx = functools.reduce(jnp.maximum, [jnp.max(s, axis=1, keepdims=True) for s in ss])
        m_old = m_ref[...]
        m_new = jnp.maximum(m_old, mx)
        alpha = jnp.exp(m_old - m_new)
        ps = [jnp.exp(s - m_new) for s in ss]
        l_ref[...] = alpha * l_ref[...] + functools.reduce(
            jnp.add, [jnp.sum(p, axis=1, keepdims=True) for p in ps])
        m_ref[...] = m_new
        for h in range(DIFF_HEADS):
            rows = slice(h * SUBLANES, (h + 1) * SUBLANES)
            acc = acc_ref[h] * alpha[rows]
            for p, vr in zip(ps, vs):
                vh = vr[pl.ds(h, PAGE_SIZE, stride=DIFF_HEADS), :].astype(bf16)
                acc = acc + jnp.dot(p[rows].astype(bf16), vh, preferred_element_type=f32)
            acc_ref[h] = acc

    process(k_refs, v_refs, None)

    @pl.when(step == pl.num_programs(1) - 1)
    def _():
        tok = lax.broadcasted_iota(jnp.int32, (nq, PAGE_SIZE), 0) % n_new
        key = lax.broadcasted_iota(jnp.int32, (nq, PAGE_SIZE), 1)
        process([kn_ref], [vn_ref], key <= tok)
        lam = _lambda(lv_ref, lam_init)
        linv = 1.0 / l_ref[...]
        for h in range(DIFF_HEADS):
            a = acc_ref[h] * linv[h * SUBLANES:(h + 1) * SUBLANES]
            o_ref[:, h * DIFF_DV:(h + 1) * DIFF_DV] = a - lam * pltpu.roll(a, n_new, 0)


def _decode(page_table, lam_vecs, qt, cache_kt, cache_v, kt_new, v_new, layer, lam_init, n_new):
    n_seq, n_pages = page_table.shape
    g = PAGES_PER_STEP
    nq = qt.shape[1]
    assert n_pages % g == 0 and 2 * n_new == SUBLANES and nq == DIFF_HEADS * SUBLANES
    kr, vr = cache_kt.shape[2], cache_v.shape[2]

    def page_spec(rows, u):
        return pl.BlockSpec((None, None, rows, LANES),
                            lambda b, s, pt: (layer, pt[b * n_pages + s * g + u], 0, 0))

    new_spec = lambda rows: pl.BlockSpec((None, rows, LANES), lambda b, s, pt: (b, 0, 0))
    grid_spec = pltpu.PrefetchScalarGridSpec(
        num_scalar_prefetch=1,
        grid=(n_seq, n_pages // g),
        in_specs=[pl.BlockSpec((None, 4, DIFF_DQK), lambda b, s, pt: (layer, 0, 0)),
                  pl.BlockSpec((None, nq, kr), lambda b, s, pt: (b, 0, 0))]
                 + [page_spec(kr, u) for u in range(g)] + [page_spec(vr, u) for u in range(g)]
                 + [new_spec(kr), new_spec(vr)],
        out_specs=pl.BlockSpec((None, SUBLANES, DIFF_HEADS * DIFF_DV), lambda b, s, pt: (b, 0, 0)),
        scratch_shapes=[pltpu.VMEM((nq, 1), f32), pltpu.VMEM((nq, 1), f32),
                        pltpu.VMEM((DIFF_HEADS, SUBLANES, DIFF_DV), f32)],
    )
    return pl.pallas_call(
        functools.partial(_decode_body, n_pages_step=g, lam_init=lam_init, n_new=n_new),
        out_shape=jax.ShapeDtypeStruct((n_seq, SUBLANES, DIFF_HEADS * DIFF_DV), f32),
        grid_spec=grid_spec,
        compiler_params=_params("parallel", "arbitrary"),
        name="decode_attn",
    )(page_table.reshape(-1), lam_vecs, qt, *([cache_kt] * g), *([cache_v] * g), kt_new, v_new)


def _merge_body(x_ref, og_ref, z_ref, od_ref, ga_ref, gb_ref, gn_ref, sn_ref, wbg_ref, wbd_ref,
                wo_ref, o_ref, yg_ref, yd_ref, *, d_scale):
    gn, sn = gn_ref[...], sn_ref[...]
    for h in range(GDN_HEADS):
        sl = slice(h * GDN_DV, (h + 1) * GDN_DV)
        zz = z_ref[:, sl]
        yg_ref[:, sl] = (_rms(og_ref[:, sl], gn) * (zz * jax.nn.sigmoid(zz))).astype(bf16)
    for h in range(DIFF_HEADS):
        sl = slice(h * DIFF_DV, (h + 1) * DIFF_DV)
        yd_ref[:, sl] = (_rms(od_ref[:, sl], sn) * d_scale).astype(bf16)
    mix = (jax.nn.sigmoid(ga_ref[...]) * jnp.dot(yg_ref[...], wbg_ref[...], preferred_element_type=f32)
           + jax.nn.sigmoid(gb_ref[...]) * jnp.dot(yd_ref[...], wbd_ref[...], preferred_element_type=f32))
    o_ref[...] = x_ref[...] + jnp.dot(mix.astype(bf16), wo_ref[...], preferred_element_type=f32)


def _merge(x, o_g, p, o_d, gn, sn, wbg, wbd, wo, layer, lam_init):
    m, d = x.shape
    tm = _pick_tile(m, 256)
    row = lambda i: (i, 0)
    wspec = lambda r, c: pl.BlockSpec((None, r, c), lambda i: (layer, 0, 0), pipeline_mode=pl.Buffered(1))
    return pl.pallas_call(
        functools.partial(_merge_body, d_scale=1.0 - lam_init),
        out_shape=jax.ShapeDtypeStruct((m, d), f32),
        grid=(m // tm,),
        in_specs=[
            pl.BlockSpec((tm, d), row),
            pl.BlockSpec((tm, GDN_V_W), row),
            pl.BlockSpec((tm, GDN_V_W), lambda i: (i, COL_Z)),
            pl.BlockSpec((tm, DIFF_W), row),
            pl.BlockSpec((tm, d), lambda i: (i, COL_GA * DIFF_W // d)),
            pl.BlockSpec((tm, d), lambda i: (i, COL_GB * DIFF_W // d)),
            pl.BlockSpec((None, 1, GDN_DV), lambda i: (layer, 0, 0)),
            pl.BlockSpec((None, 1, DIFF_DV), lambda i: (layer, 0, 0)),
            wspec(GDN_V_W, d), wspec(DIFF_W, d), wspec(d, d),
        ],
        out_specs=pl.BlockSpec((tm, d), row),
        scratch_shapes=[pltpu.VMEM((tm, GDN_V_W), bf16), pltpu.VMEM((tm, DIFF_W), bf16)],
        compiler_params=_params("parallel"),
        name="merge",
    )(x, o_g, p, o_d, p, p, gn, sn, wbg, wbd, wo)


def _final_body(x_ref, w_ref, o_ref):
    o_ref[...] = _rms(x_ref[...], w_ref[...])


def _final_norm(x, w):
    m, d = x.shape
    tm = _pick_tile(m, 768)
    return pl.pallas_call(
        _final_body,
        out_shape=jax.ShapeDtypeStruct((m, d), f32),
        grid=(m // tm,),
        in_specs=[pl.BlockSpec((tm, d), lambda i: (i, 0)), pl.BlockSpec((1, d), lambda i: (0, 0))],
        out_specs=pl.BlockSpec((tm, d), lambda i: (i, 0)),
        compiler_params=_params("parallel"),
        name="final_norm",
    )(x, w)


def kernel(x_prompt, x_sample, cache_k, cache_v, state_delta, state_conv, page_table, meta_tokens,
           ffn1_norm, ffn1_gate, ffn1_up, ffn1_down, mix_norm, w_in, conv_w, gdn_a_log, gdn_dt_bias,
           gdn_out_norm, diff_lambda, diff_subln, w_branch_gdn, w_branch_diff, w_out, ffn2_norm,
           ffn2_gate, ffn2_up, ffn2_down, final_norm):
    nb, seq, d = x_prompt.shape
    ns, ls, _ = x_sample.shape
    depth = w_in.shape[0]
    n_tok = seq + N_META
    rps = -(-n_tok // CHUNK) * CHUNK
    n_smp = ns * ls
    assert rps - n_tok >= n_smp and d == 2 * DIFF_W
    past = page_table.shape[1] * PAGE_SIZE
    n_pool = cache_k.shape[1]

    x = jnp.concatenate([jnp.broadcast_to(meta_tokens[None], (nb, N_META, d)), x_prompt,
                         jnp.zeros((nb, rps - n_tok, d), f32)], axis=1).reshape(nb * rps, d)
    s_lo, s_hi = n_tok, n_tok + n_smp
    x = x.at[s_lo:s_hi].set(x_sample.reshape(n_smp, d))

    pos = jnp.tile(jnp.arange(rps, dtype=jnp.int32), nb)
    pos = pos.at[s_lo:s_hi].set(jnp.tile(past + jnp.arange(ls, dtype=jnp.int32), ns))
    tabs = _rope_tables(pos)

    gate_lo = CONV_CH + GDN_V_W
    gate_hi = gate_lo + 2 * GDN_HEADS
    w_main = jnp.swapaxes(w_in, 1, 2).astype(bf16)
    w_tail = jnp.pad(w_main[:, gate_lo:gate_hi], ((0, 0), (0, LANES - 2 * GDN_HEADS), (0, 0)))
    cast = lambda w: w.astype(bf16)
    f1g, f1u, f1d, f2g, f2u, f2d = map(cast, (ffn1_gate, ffn1_up, ffn1_down, ffn2_gate, ffn2_up, ffn2_down))
    wbg, wbd, wo = map(cast, (w_branch_gdn, w_branch_diff, w_out))
    row3 = lambda w: w.reshape(depth, 1, -1)
    n1, n2, nm, gn, sn = map(row3, (ffn1_norm, ffn2_norm, mix_norm, gdn_out_norm, diff_subln))
    lane_pad = lambda w: jnp.pad(w, ((0, 0), (GDN_HEADS, LANES - 2 * GDN_HEADS))).reshape(depth, 1, LANES)
    alog_row, dtb_row = lane_pad(gdn_a_log), lane_pad(gdn_dt_bias)

    ck = jnp.transpose(cache_k, (0, 1, 3, 4, 5, 2)).reshape(depth, n_pool, DIFF_W, PAGE_SIZE)
    cv = cache_v.reshape(depth, n_pool, PAGE_SIZE * DIFF_HEADS, DIFF_DV)
    zero_state = jnp.zeros((nb, GDN_HEADS, GDN_DK, GDN_DV), f32)
    zero_conv = jnp.zeros((nb, SUBLANES, CONV_CH), f32)
    conv_pad = jnp.pad(state_conv, ((0, 0), (0, 0), (SUBLANES - (CONV_W - 1), 0), (0, 0)))

    def smp_chunks(a):
        w = a.shape[-1]
        return jnp.pad(a.reshape(ns, ls, w), ((0, 0), (0, CHUNK - ls), (0, 0))).reshape(ns * CHUNK, w)

    outs = [[] for _ in range(8)]
    for l in range(depth):
        lam_init = 0.8 - 0.6 * math.exp(-0.3 * l)
        x = _ffn(x, n1, f1g, f1u, f1d, l)
        p, tail = _inproj(x, nm, w_main, w_tail, l)
        q_bf, k_bf, v_bf, k_f = _rope(p, tabs)

        o_g, st_p = _gdn(p.reshape(nb, rps, -1), tail.reshape(nb, rps, LANES), conv_w, alog_row, dtb_row,
                         zero_state, zero_conv, l, n_tok)
        qkv_s = p[s_lo:s_hi, :CONV_CH]
        og_s, st_s = _gdn(smp_chunks(qkv_s).reshape(ns, CHUNK, CONV_CH),
                          smp_chunks(tail[s_lo:s_hi]).reshape(ns, CHUNK, LANES), conv_w, alog_row, dtb_row,
                          state_delta[l], conv_pad[l], l, ls)
        o_g = o_g.reshape(nb * rps, GDN_V_W).at[s_lo:s_hi].set(og_s[:, :ls].reshape(n_smp, GDN_V_W))

        o_d = _attn(diff_lambda, q_bf, k_bf, v_bf, l, lam_init, nb, rps)
        q_s = q_bf[s_lo:s_hi].reshape(ns, ls, DIFF_HEADS, 2, DIFF_DQK)
        hc = DIFF_HEADS * 2
        eye_hc = jnp.eye(hc, dtype=bf16).reshape(DIFF_HEADS, 2, 1, DIFF_HEADS, 2, 1)
        qt = (jnp.transpose(q_s, (0, 2, 3, 1, 4))[:, :, :, :, None, None, :]
              * eye_hc[None, :, :, :, :, :, :]).reshape(ns, hc * ls, DIFF_W)
        kt_new = jnp.pad(jnp.transpose(k_f[s_lo:s_hi].reshape(ns, ls, DIFF_W), (0, 2, 1)),
                         ((0, 0), (0, 0), (0, PAGE_SIZE - ls)))
        v_s = p[s_lo:s_hi, COL_DV * DIFF_W:(COL_DV + 1) * DIFF_W]
        v_new = jnp.pad(v_s.reshape(ns, ls, DIFF_HEADS, DIFF_DV),
                        ((0, 0), (0, PAGE_SIZE - ls), (0, 0), (0, 0))).reshape(ns, PAGE_SIZE * DIFF_HEADS, DIFF_DV)
        od_s = _decode(page_table, diff_lambda, qt, ck, cv, kt_new, v_new, l, lam_init, ls)
        o_d = o_d.at[s_lo:s_hi].set(od_s[:, :ls].reshape(n_smp, DIFF_W))

        x = _merge(x, o_g, p, o_d, gn, sn, wbg, wbd, wo, l, lam_init)
        x = _ffn(x, n2, f2g, f2u, f2d, l)

        seq_view = lambda a: a.reshape(nb, rps, a.shape[-1])
        outs[0].append(seq_view(k_f)[:, :n_tok].reshape(nb, n_tok, DIFF_HEADS, 2, DIFF_DQK))
        outs[1].append(seq_view(p[:, COL_DV * DIFF_W:(COL_DV + 1) * DIFF_W])[:, :n_tok]
                       .reshape(nb, n_tok, DIFF_HEADS, DIFF_DV))
        outs[2].append(st_p)
        outs[3].append(seq_view(p)[:, n_tok - (CONV_W - 1):n_tok, :CONV_CH])
        outs[4].append(k_f[s_lo:s_hi].reshape(ns, ls, DIFF_HEADS, 2, DIFF_DQK))
        outs[5].append(v_s.reshape(ns, ls, DIFF_HEADS, DIFF_DV))
        outs[6].append(st_s)
        outs[7].append(jnp.concatenate([state_conv[l], qkv_s.reshape(ns, ls, CONV_CH)], axis=1)[:, -(CONV_W - 1):])

    y = _final_norm(x, final_norm.reshape(1, d))
    y_prompt = y.reshape(nb, rps, d)[:, N_META:n_tok]
    y_sample = y[s_lo:s_hi].reshape(ns, ls, d)
    return (y_prompt, y_sample) + tuple(jnp.stack(o) for o in outs)
```
